```python
import jax, jax.numpy as jnp
from jax import lax
import numpy as np

D_MODEL = 2048
BATCH = 2
SEQ = 16384
DEPTH = 1

CONV_GROUPS = 8
CONV_GROUP_DIM = 128
CONV_DIM = CONV_GROUPS * CONV_GROUP_DIM
CONV_K = 3
MLA_HEADS = 8
QK_NOPE = 128
QK_ROPE = 64
QK_HEAD = QK_NOPE + QK_ROPE
V_HEAD = 128
MLA_DIM = MLA_HEADS * V_HEAD
D_MIX = CONV_DIM + MLA_DIM
Q_LORA = 768
KV_LORA = 512
D_IN = 3 * CONV_DIM + Q_LORA + KV_LORA + QK_ROPE
ROPE_THETA = 10000.0
N_EXPERTS = 32
TOP_K = 4
D_EXPERT = D_MODEL
SWIGLU_LIMIT = 7.0
SWIGLU_ALPHA = 1.702
MOE_BLOCK = 256
Q_BLOCK = 128
K_BLOCK = 128
EPS = 1e-6

kernel_name = "hymba_conv_mla_adaln_moe_layer"


def rms_norm(x, g):
    xf = x.astype(jnp.float32)
    y = xf * lax.rsqrt(jnp.mean(xf * xf, axis=-1, keepdims=True) + EPS)
    return (y * g.astype(jnp.float32)).astype(x.dtype)


def group_rms_norm(x, g, n_groups):
    shp = x.shape
    xg = x.reshape(shp[:-1] + (n_groups, shp[-1] // n_groups))
    return rms_norm(xg, g.reshape(n_groups, -1)).reshape(shp)


def apply_rope(x, pos):
    half = QK_ROPE // 2
    inv_freq = ROPE_THETA ** (-jnp.arange(half, dtype=jnp.float32) / half)
    ang = pos.astype(jnp.float32)[:, :, None, None] * inv_freq
    cos, sin = jnp.cos(ang), jnp.sin(ang)
    xf = x.astype(jnp.float32)
    x1, x2 = xf[..., :half], xf[..., half:]
    return jnp.concatenate([x1 * cos - x2 * sin, x2 * cos + x1 * sin], axis=-1).astype(x.dtype)


def causal_dwconv(u, w):
    return lax.conv_general_dilated(
        u, w[:, None, :].astype(u.dtype), window_strides=(1,),
        padding=[(CONV_K - 1, 0)], dimension_numbers=('NWC', 'WIO', 'NWC'),
        feature_group_count=u.shape[-1])


def causal_block_attention(q, k, v):
    B, S, H, Dq = q.shape
    Dv = v.shape[-1]
    nq = S // Q_BLOCK
    scale = Dq ** -0.5
    qb = q.reshape(B, nq, Q_BLOCK, H, Dq).transpose(1, 0, 2, 3, 4)

    def one_block(args):
        i, qblk = args
        q_pos = i * Q_BLOCK + jnp.arange(Q_BLOCK)
        qf = qblk.astype(jnp.float32) * scale

        def body(j, carry):
            m, l, acc = carry
            kblk = lax.dynamic_slice_in_dim(k, j * K_BLOCK, K_BLOCK, axis=1).astype(jnp.float32)
            vblk = lax.dynamic_slice_in_dim(v, j * K_BLOCK, K_BLOCK, axis=1).astype(jnp.float32)
            s = jnp.einsum('bqhd,bkhd->bhqk', qf, kblk)
            k_pos = j * K_BLOCK + jnp.arange(K_BLOCK)
            s = jnp.where(k_pos[None, :] <= q_pos[:, None], s, -jnp.inf)
            m_new = jnp.maximum(m, s.max(axis=-1))
            p = jnp.exp(s - m_new[..., None])
            corr = jnp.exp(m - m_new)
            l = l * corr + p.sum(axis=-1)
            acc = acc * corr[..., None] + jnp.einsum('bhqk,bkhd->bhqd', p, vblk)
            return m_new, l, acc

        init = (jnp.full((B, H, Q_BLOCK), -jnp.inf, jnp.float32),
                jnp.zeros((B, H, Q_BLOCK), jnp.float32),
                jnp.zeros((B, H, Q_BLOCK, Dv), jnp.float32))
        n_kv = ((i + 1) * Q_BLOCK + K_BLOCK - 1) // K_BLOCK
        m, l, acc = lax.fori_loop(0, n_kv, body, init)
        out = acc / l[..., None]
        return out.transpose(0, 2, 1, 3).astype(q.dtype)

    out = lax.map(one_block, (jnp.arange(nq), qb))
    return out.transpose(1, 0, 2, 3, 4).reshape(B, S, H, Dv)


def hybrid_mixer(h, positions, w_in, conv_w, q_lat_norm_g, w_uq, kv_lat_norm_g, w_ukv,
                 q_head_norm_g, k_head_norm_g, conv_out_norm_g, mla_out_norm_g, w_out):
    B, S, _ = h.shape
    proj = jnp.einsum('bsd,de->bse', h, w_in)
    splits = [CONV_DIM, 2 * CONV_DIM, 3 * CONV_DIM, 3 * CONV_DIM + Q_LORA,
              3 * CONV_DIM + Q_LORA + KV_LORA]
    b_gate, c_gate, u, q_lat, kv_lat, k_pe = jnp.split(proj, splits, axis=-1)

    y_conv = b_gate * causal_dwconv(c_gate * u, conv_w)

    q = jnp.einsum('bsr,re->bse', rms_norm(q_lat, q_lat_norm_g), w_uq)
    q = q.reshape(B, S, MLA_HEADS, QK_HEAD)
    kv = jnp.einsum('bsr,re->bse', rms_norm(kv_lat, kv_lat_norm_g), w_ukv)
    kv = kv.reshape(B, S, MLA_HEADS, QK_NOPE + V_HEAD)
    k_nope, v = kv[..., :QK_NOPE], kv[..., QK_NOPE:]
    k_pe = jnp.broadcast_to(k_pe[:, :, None, :], (B, S, MLA_HEADS, QK_ROPE))
    k = jnp.concatenate([k_nope, k_pe], axis=-1)
    q = rms_norm(q, q_head_norm_g)
    k = rms_norm(k, k_head_norm_g)
    q = jnp.concatenate([q[..., :QK_NOPE], apply_rope(q[..., QK_NOPE:], positions)], axis=-1)
    k = jnp.concatenate([k[..., :QK_NOPE], apply_rope(k[..., QK_NOPE:], positions)], axis=-1)
    y_mla = causal_block_attention(q, k, v).reshape(B, S, MLA_DIM)

    mixed = jnp.concatenate([group_rms_norm(y_conv, conv_out_norm_g, CONV_GROUPS),
                             group_rms_norm(y_mla, mla_out_norm_g, MLA_HEADS)], axis=-1)
    return jnp.einsum('bse,ed->bsd', mixed, w_out)


def moe_ffn(h, router_w, router_b, w_gate_up, b_gate_up, w_down, b_down):
    T, D = h.shape
    logits = (jnp.einsum('td,de->te', h, router_w) + router_b).astype(jnp.float32)
    top_v, top_i = lax.top_k(logits, TOP_K)
    gates = jax.nn.softmax(top_v, axis=-1)

    TK = T * TOP_K
    flat_e = top_i.reshape(-1)
    order = jnp.argsort(flat_e)
    sorted_e = flat_e[order]
    tok = (order // TOP_K).astype(jnp.int32)
    counts = jnp.bincount(flat_e, length=N_EXPERTS)
    padded = ((counts + MOE_BLOCK - 1) // MOE_BLOCK) * MOE_BLOCK
    padded_end = jnp.cumsum(padded)
    padded_start = padded_end - padded
    group_start = jnp.cumsum(counts) - counts
    rank = jnp.arange(TK) - group_start[sorted_e]
    dest = padded_start[sorted_e] + rank
    n_blocks = -(-TK // MOE_BLOCK) + N_EXPERTS
    P = n_blocks * MOE_BLOCK
    buf_tok = jnp.zeros((P,), jnp.int32).at[dest].set(tok).reshape(n_blocks, MOE_BLOCK)
    block_start = jnp.arange(n_blocks) * MOE_BLOCK
    block_expert = jnp.minimum(jnp.searchsorted(padded_end, block_start, side='right'),
                               N_EXPERTS - 1)

    def expert_block(args):
        rows, e = args
        xe = h[rows]
        gu = xe @ w_gate_up[e] + b_gate_up[e]
        g, up = gu[:, :D_EXPERT], gu[:, D_EXPERT:]
        g = jnp.minimum(g, SWIGLU_LIMIT)
        up = jnp.clip(up, -SWIGLU_LIMIT, SWIGLU_LIMIT)
        act = (up + 1.0) * g * jax.nn.sigmoid(SWIGLU_ALPHA * g)
        return act @ w_down[e] + b_down[e]

    y_buf = lax.map(expert_block, (buf_tok, block_expert)).reshape(P, D)
    dest_orig = jnp.zeros((TK,), dest.dtype).at[order].set(dest)
    y = y_buf[dest_orig].reshape(T, TOP_K, D)
    return jnp.einsum('tkd,tk->td', y, gates.astype(y.dtype))


def setup_inputs(seed: int = 0) -> dict:
    key = jax.random.key(seed)
    ks = jax.random.split(key, 24)
    f32 = jnp.float32
    L = DEPTH

    def nrm(k, shape, scale):
        return jax.random.normal(k, shape, f32) * scale

    def gain(k, shape):
        return 1.0 + 0.05 * jax.random.normal(k, shape, f32)

    x = nrm(ks[0], (BATCH, SEQ, D_MODEL), 1.0)
    c = nrm(ks[1], (BATCH, D_MODEL), 1.0)
    positions = (jax.random.randint(ks[2], (BATCH, 1), 0, 4096, jnp.int32)
                 + jnp.arange(SEQ, dtype=jnp.int32)[None, :])
    return {
        "x": x,
        "c": c,
        "positions": positions,
        "w_ada": nrm(ks[3], (L, D_MODEL, 6 * D_MODEL), 0.5 * D_MODEL ** -0.5),
        "b_ada": nrm(ks[4], (L, 6 * D_MODEL), 0.02),
        "norm_mix_g": gain(ks[5], (L, D_MODEL)),
        "w_in": nrm(ks[6], (L, D_MODEL, D_IN), D_MODEL ** -0.5),
        "conv_w": nrm(ks[7], (L, CONV_K, CONV_DIM), CONV_K ** -0.5),
        "q_lat_norm_g": gain(ks[8], (L, Q_LORA)),
        "w_uq": nrm(ks[9], (L, Q_LORA, MLA_HEADS * QK_HEAD), Q_LORA ** -0.5),
        "kv_lat_norm_g": gain(ks[10], (L, KV_LORA)),
        "w_ukv": nrm(ks[11], (L, KV_LORA, MLA_HEADS * (QK_NOPE + V_HEAD)), KV_LORA ** -0.5),
        "q_head_norm_g": gain(ks[12], (L, QK_HEAD)),
        "k_head_norm_g": gain(ks[13], (L, QK_HEAD)),
        "conv_out_norm_g": gain(ks[14], (L, CONV_DIM)),
        "mla_out_norm_g": gain(ks[15], (L, MLA_DIM)),
        "w_out": nrm(ks[16], (L, D_MIX, D_MODEL), D_MIX ** -0.5),
        "norm_ffn_g": gain(ks[17], (L, D_MODEL)),
        "router_w": nrm(ks[18], (L, D_MODEL, N_EXPERTS), D_MODEL ** -0.5),
        "router_b": nrm(ks[19], (L, N_EXPERTS), 0.01),
        "w_gate_up": nrm(ks[20], (L, N_EXPERTS, D_MODEL, 2 * D_EXPERT), D_MODEL ** -0.5),
        "b_gate_up": nrm(ks[21], (L, N_EXPERTS, 2 * D_EXPERT), 0.01),
        "w_down": nrm(ks[22], (L, N_EXPERTS, D_EXPERT, D_MODEL), D_EXPERT ** -0.5),
        "b_down": nrm(ks[23], (L, N_EXPERTS, D_MODEL), 0.01),
    }


def reference(x, c, positions, w_ada, b_ada, norm_mix_g, w_in, conv_w, q_lat_norm_g, w_uq,
              kv_lat_norm_g, w_ukv, q_head_norm_g, k_head_norm_g, conv_out_norm_g,
              mla_out_norm_g, w_out, norm_ffn_g, router_w, router_b, w_gate_up, b_gate_up,
              w_down, b_down):
    B, S, D = x.shape
    cond = jax.nn.silu(c)
    for l in range(DEPTH):
        mod = jnp.einsum('bd,de->be', cond, w_ada[l]) + b_ada[l]
        sh_m, sc_m, g_m, sh_f, sc_f, g_f = jnp.split(mod[:, None, :], 6, axis=-1)
        h = rms_norm(x, norm_mix_g[l]) * (1.0 + sc_m) + sh_m
        x = x + g_m * hybrid_mixer(h, positions, w_in[l], conv_w[l], q_lat_norm_g[l], w_uq[l],
                                   kv_lat_norm_g[l], w_ukv[l], q_head_norm_g[l],
                                   k_head_norm_g[l], conv_out_norm_g[l], mla_out_norm_g[l],
                                   w_out[l])
        h = rms_norm(x, norm_ffn_g[l]) * (1.0 + sc_f) + sh_f
        y = moe_ffn(h.reshape(B * S, D), router_w[l], router_b[l], w_gate_up[l], b_gate_up[l],
                    w_down[l], b_down[l])
        x = x + g_f * y.reshape(B, S, D)
    return x
```

```python
import functools

import jax
import jax.numpy as jnp
from jax import lax
from jax.experimental import pallas as pl
from jax.experimental.pallas import tpu as pltpu

F32 = jnp.float32
BF16 = jnp.bfloat16

D_MODEL = 2048
CONV_GROUPS = 8
CONV_DIM = 1024
CONV_K = 3
MLA_HEADS = 8
QK_NOPE = 128
QK_ROPE = 64
QK_HEAD = QK_NOPE + QK_ROPE
V_HEAD = 128
MLA_DIM = MLA_HEADS * V_HEAD
Q_LORA = 768
KV_LORA = 512
ROPE_THETA = 10000.0
N_EXPERTS = 32
TOP_K = 4
D_EXPERT = D_MODEL
SWIGLU_LIMIT = 7.0
SWIGLU_ALPHA = 1.702
EPS = 1e-6

LANES = 128
QK_PAD = 2 * LANES
_C_B, _C_C, _C_U = 0, CONV_DIM, 2 * CONV_DIM
_C_Q = 3 * CONV_DIM
_C_KV = _C_Q + Q_LORA
_C_PE = _C_KV + KV_LORA
D_IN_PAD = _C_PE + LANES
VMEM_LIMIT = 56 * 1024 * 1024


def _cparams(sem):
    return pltpu.CompilerParams(dimension_semantics=sem, vmem_limit_bytes=VMEM_LIMIT)


def _resident(shape):
    nd = len(shape)
    return pl.BlockSpec(shape, lambda *_: (0,) * nd, pipeline_mode=pl.Buffered(1))


def _split_bf16(a):
    hi = a.astype(BF16)
    lo = (a - hi.astype(F32)).astype(BF16)
    return hi, lo


def _dot(a, b):
    return jnp.dot(a, b, preferred_element_type=F32)


def _dot_nt(a, b):
    return lax.dot_general(a, b, (((1,), (1,)), ((), ())), preferred_element_type=F32)


def _rms(x, n):
    return lax.rsqrt(jnp.sum(x * x, axis=-1, keepdims=True) * (1.0 / n) + EPS)


def _ada_kernel(c_ref, w_ref, b_ref, o_ref):
    c = c_ref[...]
    cond = c * jax.nn.sigmoid(c)
    c_hi, c_lo = _split_bf16(cond)
    w_hi, w_lo = _split_bf16(w_ref[...])
    o_ref[...] = _dot(c_hi, w_hi) + _dot(c_hi, w_lo) + _dot(c_lo, w_hi) + b_ref[...]


def _ada(c, w_ada, b_ada):
    B, D = c.shape
    n_out = w_ada.shape[1]
    rows = 8
    c_pad = jnp.zeros((rows, D), F32).at[:B].set(c)
    tn = 1024
    out = pl.pallas_call(
        _ada_kernel,
        grid=(n_out // tn,),
        in_specs=[pl.BlockSpec((rows, D), lambda j: (0, 0)),
                  pl.BlockSpec((D, tn), lambda j: (0, j)),
                  pl.BlockSpec((1, tn), lambda j: (0, j))],
        out_specs=pl.BlockSpec((rows, tn), lambda j: (0, j)),
        out_shape=jax.ShapeDtypeStruct((rows, n_out), F32),
        compiler_params=_cparams(("arbitrary",)),
        name="ada",
    )(c_pad, w_ada, b_ada.reshape(1, n_out))
    return out[:B]


def _inproj_kernel(x_ref, mod_ref, g_ref, w_ref, cw_ref, cg_ref, qg_ref, kvg_ref,
                   mc_ref, qln_ref, kvln_ref, kpe_ref, cu_scr, *, tm, tiles_per_seq):
    i = pl.program_id(0)
    x = x_ref[...]
    mod = mod_ref[0]
    sh, sc = mod[0:1], mod[1:2]
    h = (x * _rms(x, D_MODEL) * g_ref[...]) * (1.0 + sc) + sh
    hb = h.astype(BF16)

    def proj(lo, hi):
        return _dot(hb, w_ref[:, lo:hi])

    cu = proj(_C_C, _C_U) * proj(_C_U, _C_Q)

    @pl.when(i % tiles_per_seq == 0)
    def _():
        cu_scr[0:8, :] = jnp.zeros((8, CONV_DIM), F32)

    cu_scr[8:8 + tm, :] = cu
    s1 = cu_scr[7:7 + tm, :]
    s2 = cu_scr[6:6 + tm, :]
    cw = cw_ref[...]
    conv = cw[0:1] * s2 + cw[1:2] * s1 + cw[2:3] * cu
    cu_scr[0:8, :] = cu_scr[tm:tm + 8, :]
    y = proj(_C_B, _C_C) * conv
    cg = cg_ref[...]
    for g in range(CONV_GROUPS):
        sl = slice(g * LANES, (g + 1) * LANES)
        yg = y[:, sl]
        mc_ref[:, sl] = (yg * _rms(yg, LANES) * cg[:, sl]).astype(BF16)

    ql = proj(_C_Q, _C_KV)
    qln_ref[...] = (ql * _rms(ql, Q_LORA) * qg_ref[...]).astype(BF16)
    kvl = proj(_C_KV, _C_PE)
    kvln_ref[...] = (kvl * _rms(kvl, KV_LORA) * kvg_ref[...]).astype(BF16)
    kpe_ref[...] = proj(_C_PE, D_IN_PAD)


def _inproj(x2, mod3, norm_g, w_in_p, conv_w, conv_g, q_lat_g, kv_lat_g, S, tm):
    T, D = x2.shape
    tps = S // tm
    row = lambda n: pl.BlockSpec((tm, n), lambda i: (i, 0))
    return pl.pallas_call(
        functools.partial(_inproj_kernel, tm=tm, tiles_per_seq=tps),
        grid=(T // tm,),
        in_specs=[row(D),
                  pl.BlockSpec((1, 6, D), lambda i: (i // tps, 0, 0)),
                  _resident((1, D)),
                  _resident((D, D_IN_PAD)),
                  _resident((CONV_K, CONV_DIM)),
                  _resident((1, CONV_DIM)),
                  _resident((1, Q_LORA)),
                  _resident((1, KV_LORA))],
        out_specs=[row(CONV_DIM), row(Q_LORA), row(KV_LORA), row(LANES)],
        out_shape=[jax.ShapeDtypeStruct((T, CONV_DIM), BF16),
                   jax.ShapeDtypeStruct((T, Q_LORA), BF16),
                   jax.ShapeDtypeStruct((T, KV_LORA), BF16),
                   jax.ShapeDtypeStruct((T, LANES), F32)],
        scratch_shapes=[pltpu.VMEM((tm + 8, CONV_DIM), F32)],
        compiler_params=_cparams(("arbitrary",)),
        name="inproj",
    )(x2, mod3, norm_g, w_in_p, conv_w, conv_g, q_lat_g, kv_lat_g)


def _upproj_kernel(qln_ref, kvln_ref, kpe_ref, pos_ref, invf_ref, wuq_ref, wukv_ref,
                   qg_ref, kg_ref, q_ref, k_ref, v_ref, *, tm):
    ang = pos_ref[...].astype(F32) * invf_ref[...]
    lane = lax.broadcasted_iota(jnp.int32, (tm, LANES), 1)
    cos = jnp.cos(ang)
    sin_signed = jnp.where(lane < LANES // 2, -1.0, 1.0) * jnp.sin(ang)
    even_head = (lane // (QK_ROPE // 2)) % 2 == 0

    def rope(col):
        return col * cos + pltpu.roll(col, LANES // 2, axis=1) * sin_signed

    qg = qg_ref[...]
    kg = kg_ref[...]
    qscale = QK_HEAD ** -0.5

    q = _dot(qln_ref[...], wuq_ref[...])
    for j in range(MLA_HEADS // 2):
        rc = q[:, MLA_DIM + j * LANES:MLA_DIM + (j + 1) * LANES]
        rc2 = rc * rc
        ss_even = jnp.sum(jnp.where(even_head, rc2, 0.0), axis=-1, keepdims=True)
        ss_odd = jnp.sum(jnp.where(even_head, 0.0, rc2), axis=-1, keepdims=True)
        r = []
        for h, ss_rope in ((2 * j, ss_even), (2 * j + 1, ss_odd)):
            qn = q[:, h * LANES:(h + 1) * LANES]
            ss = jnp.sum(qn * qn, axis=-1, keepdims=True) + ss_rope
            rh = lax.rsqrt(ss * (1.0 / QK_HEAD) + EPS) * qscale
            r.append(rh)
            q_ref[0, h, :, 0:LANES] = (qn * rh * qg[:, 0:LANES]).astype(BF16)
        roped = rope(rc * jnp.where(even_head, r[0], r[1]) * qg[:, LANES:2 * LANES]).astype(BF16)
        q_ref[0, 2 * j, :, LANES:2 * LANES] = roped
        q_ref[0, 2 * j + 1, :, LANES:2 * LANES] = roped

    kv = _dot(kvln_ref[...], wukv_ref[...])
    kp = kpe_ref[...]
    ss_pe = 0.5 * jnp.sum(kp * kp, axis=-1, keepdims=True)
    kr = rope(kp * kg[:, LANES:2 * LANES])
    for h in range(MLA_HEADS):
        kn = kv[:, h * LANES:(h + 1) * LANES]
        ss = jnp.sum(kn * kn, axis=-1, keepdims=True) + ss_pe
        rh = lax.rsqrt(ss * (1.0 / QK_HEAD) + EPS)
        k_ref[0, h, :, 0:LANES] = (kn * rh * kg[:, 0:LANES]).astype(BF16)
        mine = even_head if h % 2 == 0 else jnp.logical_not(even_head)
        k_ref[0, h, :, LANES:2 * LANES] = jnp.where(mine, kr * rh, 0.0).astype(BF16)
        v_ref[0, h] = kv[:, MLA_DIM + h * LANES:MLA_DIM + (h + 1) * LANES].astype(BF16)


def _upproj(qln, kvln, kpe, pos_col, invf, wuq_p, wukv_p, qg_p, kg_p, B, S, tm):
    T = qln.shape[0]
    tps = S // tm
    row = lambda n: pl.BlockSpec((tm, n), lambda i: (i, 0))
    head = lambda n: pl.BlockSpec((1, MLA_HEADS, tm, n), lambda i: (i // tps, 0, i % tps, 0))
    return pl.pallas_call(
        functools.partial(_upproj_kernel, tm=tm),
        grid=(T // tm,),
        in_specs=[row(Q_LORA), row(KV_LORA), row(LANES), row(1),
                  _resident((1, LANES)),
                  _resident(wuq_p.shape), _resident(wukv_p.shape),
                  _resident((1, 2 * LANES)), _resident((1, 2 * LANES))],
        out_specs=[head(QK_PAD), head(QK_PAD), head(V_HEAD)],
        out_shape=[jax.ShapeDtypeStruct((B, MLA_HEADS, S, QK_PAD), BF16),
                   jax.ShapeDtypeStruct((B, MLA_HEADS, S, QK_PAD), BF16),
                   jax.ShapeDtypeStruct((B, MLA_HEADS, S, V_HEAD), BF16)],
        compiler_params=_cparams(("arbitrary",)),
        name="upproj",
    )(qln, kvln, kpe, pos_col, invf, wuq_p, wukv_p, qg_p, kg_p)


def _attn_kernel(q_ref, k_ref, v_ref, g_ref, o_ref, m_scr, l_scr, acc_scr, *, tq, tk):
    qi = pl.program_id(1)
    q = q_ref[0]
    m_scr[...] = jnp.full((tq, 1), -jnp.inf, F32)
    l_scr[...] = jnp.zeros((tq, 1), F32)
    acc_scr[...] = jnp.zeros((tq, V_HEAD), F32)

    def step(j, masked):
        start = pl.multiple_of(j * tk, tk)
        k = k_ref[0, pl.ds(start, tk), :]
        v = v_ref[0, pl.ds(start, tk), :]
        s = _dot_nt(q, k)
        if masked:
            row = lax.broadcasted_iota(jnp.int32, (tq, tk), 0) + qi * tq
            col = lax.broadcasted_iota(jnp.int32, (tq, tk), 1) + j * tk
            s = jnp.where(col <= row, s, -jnp.inf)
        m_prev = m_scr[...]
        m_new = jnp.maximum(m_prev, jnp.max(s, axis=-1, keepdims=True))
        p = jnp.exp(s - m_new)
        corr = jnp.exp(m_prev - m_new)
        l_scr[...] = corr * l_scr[...] + jnp.sum(p, axis=-1, keepdims=True)
        acc_scr[...] = corr * acc_scr[...] + _dot(p.astype(BF16), v)
        m_scr[...] = m_new

    n_full = qi * (tq // tk)

    def body(j, carry):
        step(j, False)
        return carry

    lax.fori_loop(0, n_full, body, 0)
    for d in range(tq // tk):
        step(n_full + d, True)

    o = acc_scr[...] / l_scr[...]
    o_ref[0] = (o * _rms(o, V_HEAD) * g_ref[0]).astype(BF16)


def _attention(q, k, v, mla_g, B, S, tq, tk):
    BH = B * MLA_HEADS
    H = MLA_HEADS
    return pl.pallas_call(
        functools.partial(_attn_kernel, tq=tq, tk=tk),
        grid=(BH, S // tq),
        in_specs=[pl.BlockSpec((1, tq, QK_PAD), lambda bh, qi: (bh, qi, 0)),
                  pl.BlockSpec((1, S, QK_PAD), lambda bh, qi: (bh, 0, 0)),
                  pl.BlockSpec((1, S, V_HEAD), lambda bh, qi: (bh, 0, 0)),
                  pl.BlockSpec((1, 1, V_HEAD), lambda bh, qi: (bh % H, 0, 0))],
        out_specs=pl.BlockSpec((1, tq, V_HEAD), lambda bh, qi: (bh // H, qi, bh % H)),
        out_shape=jax.ShapeDtypeStruct((B, S, MLA_DIM), BF16),
        scratch_shapes=[pltpu.VMEM((tq, 1), F32), pltpu.VMEM((tq, 1), F32),
                        pltpu.VMEM((tq, V_HEAD), F32)],
        compiler_params=_cparams(("arbitrary", "arbitrary")),
        name="attn",
    )(q.reshape(BH, S, QK_PAD), k.reshape(BH, S, QK_PAD), v.reshape(BH, S, V_HEAD),
      mla_g.reshape(H, 1, V_HEAD))


def _outproj_kernel(mc_ref, mm_ref, x_ref, mod_ref, w_ref, g_ref, rw_ref, rb_ref,
                    x1_ref, h2_ref, idx_ref, gate_ref, *, tm):
    mod = mod_ref[0]
    g_m, sh_f, sc_f = mod[2:3], mod[3:4], mod[4:5]
    mix = _dot(mc_ref[...], w_ref[0:CONV_DIM, :]) + _dot(mm_ref[...], w_ref[CONV_DIM:, :])
    x1 = x_ref[...] + g_m * mix
    x1_ref[...] = x1
    h2 = (x1 * _rms(x1, D_MODEL) * g_ref[...]) * (1.0 + sc_f) + sh_f
    h2_ref[...] = h2

    h_hi, h_lo = _split_bf16(h2)
    r_hi, r_lo = _split_bf16(rw_ref[...])
    logits = _dot_nt(r_hi, h_hi) + _dot_nt(r_lo, h_hi) + _dot_nt(r_hi, h_lo) + rb_ref[...]
    eid = lax.broadcasted_iota(jnp.int32, (N_EXPERTS, tm), 0)
    vals = []
    for kk in range(TOP_K):
        mx = jnp.max(logits, axis=0, keepdims=True)
        idx = jnp.min(jnp.where(logits == mx, eid, N_EXPERTS), axis=0, keepdims=True)
        idx_ref[kk:kk + 1, :] = idx
        vals.append(mx)
        logits = jnp.where(eid == idx, -jnp.inf, logits)
    ex = [jnp.exp(vv - vals[0]) for vv in vals]
    denom = ex[0] + ex[1] + ex[2] + ex[3]
    for kk in range(TOP_K):
        gate_ref[kk:kk + 1, :] = ex[kk] / denom


def _outproj(mc, mm, x2, mod3, w_out_b, norm_g, rw_t, rb_col, S, tm):
    T, D = x2.shape
    tps = S // tm
    row = lambda n: pl.BlockSpec((tm, n), lambda i: (i, 0))
    colblk = lambda n: pl.BlockSpec((n, tm), lambda i: (0, i))
    return pl.pallas_call(
        functools.partial(_outproj_kernel, tm=tm),
        grid=(T // tm,),
        in_specs=[row(CONV_DIM), row(MLA_DIM), row(D),
                  pl.BlockSpec((1, 6, D), lambda i: (i // tps, 0, 0)),
                  _resident((D, D)), _resident((1, D)),
                  _resident((N_EXPERTS, D)), _resident((N_EXPERTS, 1))],
        out_specs=[row(D), row(D), colblk(TOP_K), colblk(TOP_K)],
        out_shape=[jax.ShapeDtypeStruct((T, D), F32),
                   jax.ShapeDtypeStruct((T, D), F32),
                   jax.ShapeDtypeStruct((TOP_K, T), jnp.int32),
                   jax.ShapeDtypeStruct((TOP_K, T), F32)],
        compiler_params=_cparams(("arbitrary",)),
        name="outproj",
    )(mc, mm, x2, mod3, w_out_b, norm_g, rw_t, rb_col)


def _gather_rows(idx_at, n_rows, src_hbm, dst_at, sem):
    def copy(r, src_row):
        return pltpu.make_async_copy(src_hbm.at[pl.ds(src_row, 1)], dst_at(r), sem)

    def issue(r, c):
        copy(r, idx_at(r)).start()
        return c

    def wait(r, c):
        copy(r, 0).wait()
        return c

    lax.fori_loop(0, n_rows, issue, 0)
    lax.fori_loop(0, n_rows, wait, 0)


def _moe_kernel(te_ref, nu_ref, tok_ref, h_hbm, wg_ref, wu_ref, bg_ref, bu_ref, wd_ref, bd_ref,
                o_ref, xbuf, acc, sem, *, tme, nf):
    i = pl.program_id(0)
    f = pl.program_id(1)

    @pl.when(i < nu_ref[0])
    def _():
        @pl.when(f == 0)
        def _():
            _gather_rows(lambda r: tok_ref[0, 0, r], tme, h_hbm,
                         lambda r: xbuf.at[pl.ds(r, 1)], sem)

        x = xbuf[...].astype(BF16)
        g = _dot(x, wg_ref[0].astype(BF16)) + bg_ref[0]
        u = _dot(x, wu_ref[0].astype(BF16)) + bu_ref[0]
        g = jnp.minimum(g, SWIGLU_LIMIT)
        u = jnp.clip(u, -SWIGLU_LIMIT, SWIGLU_LIMIT)
        act = (u + 1.0) * g * jax.nn.sigmoid(SWIGLU_ALPHA * g)
        part = _dot(act.astype(BF16), wd_ref[0].astype(BF16))

        @pl.when(f == 0)
        def _():
            acc[...] = part + bd_ref[0]

        @pl.when(f > 0)
        def _():
            acc[...] += part

        @pl.when(f == nf - 1)
        def _():
            o_ref[...] = acc[...]

    @pl.when(jnp.logical_and(i >= nu_ref[0], f == nf - 1))
    def _():
        o_ref[...] = jnp.zeros((tme, o_ref.shape[1]), F32)


def _moe(tile_expert, n_used, buf_tok, h2, w_gu, b_gu, w_d, b_d, tme, tf):
    n_tiles = tile_expert.shape[0]
    D = h2.shape[1]
    nf = D_EXPERT // tf

    def fcl(i, f, nu):
        return jnp.where(i < nu[0], f, nf - 1)

    grid_spec = pltpu.PrefetchScalarGridSpec(
        num_scalar_prefetch=2,
        grid=(n_tiles, nf),
        in_specs=[
            pl.BlockSpec((1, 1, tme), lambda i, f, te, nu: (i, 0, 0), memory_space=pltpu.SMEM),
            pl.BlockSpec(memory_space=pl.ANY),
            pl.BlockSpec((1, D, tf), lambda i, f, te, nu: (te[i], 0, fcl(i, f, nu))),
            pl.BlockSpec((1, D, tf), lambda i, f, te, nu: (te[i], 0, nf + fcl(i, f, nu))),
            pl.BlockSpec((1, 1, tf), lambda i, f, te, nu: (te[i], 0, fcl(i, f, nu))),
            pl.BlockSpec((1, 1, tf), lambda i, f, te, nu: (te[i], 0, nf + fcl(i, f, nu))),
            pl.BlockSpec((1, tf, D), lambda i, f, te, nu: (te[i], fcl(i, f, nu), 0)),
            pl.BlockSpec((1, 1, D), lambda i, f, te, nu: (te[i], 0, 0)),
        ],
        out_specs=pl.BlockSpec((tme, D), lambda i, f, te, nu: (i, 0)),
        scratch_shapes=[pltpu.VMEM((tme, D), F32), pltpu.VMEM((tme, D), F32),
                        pltpu.SemaphoreType.DMA(())],
    )
    return pl.pallas_call(
        functools.partial(_moe_kernel, tme=tme, nf=nf),
        grid_spec=grid_spec,
        out_shape=jax.ShapeDtypeStruct((n_tiles * tme, D), F32),
        compiler_params=_cparams(("arbitrary", "arbitrary")),
        name="moe",
    )(tile_expert, n_used, buf_tok.reshape(n_tiles, 1, tme), h2,
      w_gu, w_gu, b_gu.reshape(N_EXPERTS, 1, 2 * D_EXPERT), b_gu.reshape(N_EXPERTS, 1, 2 * D_EXPERT),
      w_d, b_d.reshape(N_EXPERTS, 1, D))


def _combine_kernel(dest_ref, gate_ref, x1_ref, mod_ref, y_hbm, o_ref, ybuf, sem, *, tm):
    for kk in range(TOP_K):
        _gather_rows(lambda r, kk=kk: dest_ref[0, kk, r], tm, y_hbm,
                     lambda r, kk=kk: ybuf.at[kk, pl.ds(r, 1)], sem)
    gates = gate_ref[...]
    y = gates[:, 0:1] * ybuf[0]
    for kk in range(1, TOP_K):
        y = y + gates[:, kk:kk + 1] * ybuf[kk]
    g_f = mod_ref[0][5:6]
    o_ref[...] = x1_ref[...] + g_f * y


def _combine(dest_tiles, gates_t, x1, mod3, y_sorted, S, tm):
    T, D = x1.shape
    tps = S // tm
    return pl.pallas_call(
        functools.partial(_combine_kernel, tm=tm),
        grid=(T // tm,),
        in_specs=[pl.BlockSpec((1, TOP_K, tm), lambda i: (i, 0, 0), memory_space=pltpu.SMEM),
                  pl.BlockSpec((tm, TOP_K), lambda i: (i, 0)),
                  pl.BlockSpec((tm, D), lambda i: (i, 0)),
                  pl.BlockSpec((1, 6, D), lambda i: (i // tps, 0, 0)),
                  pl.BlockSpec(memory_space=pl.ANY)],
        out_specs=pl.BlockSpec((tm, D), lambda i: (i, 0)),
        out_shape=jax.ShapeDtypeStruct((T, D), F32),
        scratch_shapes=[pltpu.VMEM((TOP_K, tm, D), F32), pltpu.SemaphoreType.DMA(())],
        compiler_params=_cparams(("arbitrary",)),
        name="combine",
    )(dest_tiles, gates_t, x1, mod3, y_sorted)


def _route(top_i, tme):
    T = top_i.shape[1]
    TK = T * TOP_K
    flat_e = top_i.T.reshape(-1)
    order = jnp.argsort(flat_e)
    sorted_e = flat_e[order]
    tok = (order // TOP_K).astype(jnp.int32)
    counts = jnp.bincount(flat_e, length=N_EXPERTS)
    padded = ((counts + tme - 1) // tme) * tme
    padded_end = jnp.cumsum(padded)
    padded_start = padded_end - padded
    group_start = jnp.cumsum(counts) - counts
    rank = jnp.arange(TK) - group_start[sorted_e]
    dest = (padded_start[sorted_e] + rank).astype(jnp.int32)
    n_tiles = -(-TK // tme) + N_EXPERTS
    buf_tok = jnp.zeros((n_tiles * tme,), jnp.int32).at[dest].set(tok)
    tile_start = jnp.arange(n_tiles) * tme
    tile_expert = jnp.minimum(jnp.searchsorted(padded_end, tile_start, side='right'),
                              N_EXPERTS - 1).astype(jnp.int32)
    n_used = (padded_end[-1] // tme).astype(jnp.int32).reshape(1)
    dest_orig = jnp.zeros((TK,), jnp.int32).at[order].set(dest)
    return tile_expert, n_used, buf_tok, dest_orig


def _permute_weights(w_in, w_uq, w_ukv, q_head_g, k_head_g):
    half = QK_ROPE // 2
    pe0 = _C_PE
    pe_cols = jnp.concatenate([jnp.arange(pe0, pe0 + half), jnp.arange(pe0, pe0 + half),
                               jnp.arange(pe0 + half, pe0 + 2 * half), jnp.arange(pe0 + half, pe0 + 2 * half)])
    w_in_p = jnp.concatenate([w_in[:, :pe0], w_in[:, pe_cols]], axis=1).astype(BF16)

    hq = jnp.arange(MLA_HEADS)[:, None] * QK_HEAD
    nope_cols = (hq + jnp.arange(QK_NOPE)[None, :]).reshape(-1)
    rope_cols = []
    for j in range(MLA_HEADS // 2):
        for part in range(2):
            for h in (2 * j, 2 * j + 1):
                rope_cols.append(h * QK_HEAD + QK_NOPE + part * half + jnp.arange(half))
    wuq_p = w_uq[:, jnp.concatenate([nope_cols] + rope_cols)].astype(BF16)

    hk = jnp.arange(MLA_HEADS)[:, None] * (QK_NOPE + V_HEAD)
    kn_cols = (hk + jnp.arange(QK_NOPE)[None, :]).reshape(-1)
    v_cols = (hk + QK_NOPE + jnp.arange(V_HEAD)[None, :]).reshape(-1)
    wukv_p = w_ukv[:, jnp.concatenate([kn_cols, v_cols])].astype(BF16)

    def gain_p(g):
        x1, x2 = g[QK_NOPE:QK_NOPE + half], g[QK_NOPE + half:]
        return jnp.concatenate([g[:QK_NOPE], x1, x1, x2, x2]).reshape(1, 2 * LANES)

    return w_in_p, wuq_p, wukv_p, gain_p(q_head_g), gain_p(k_head_g)


def _tile(n, pref):
    t = pref
    while n % t:
        t //= 2
    return t


def kernel(x, c, positions, w_ada, b_ada, norm_mix_g, w_in, conv_w, q_lat_norm_g, w_uq, kv_lat_norm_g, w_ukv, q_head_norm_g, k_head_norm_g, conv_out_norm_g, mla_out_norm_g, w_out, norm_ffn_g, router_w, router_b, w_gate_up, b_gate_up, w_down, b_down):
    B, S, D = x.shape
    T = B * S
    assert D == D_MODEL and w_ada.shape[0] == 1 and S % 128 == 0
    l = 0
    tm = _tile(S, 512)
    tq = _tile(S, 512)
    tme = 1024 if T * TOP_K >= 32768 else 256
    tmc = _tile(S, 256)

    mod3 = _ada(c, w_ada[l], b_ada[l]).reshape(B, 6, D)
    x2 = x.reshape(T, D)

    w_in_p, wuq_p, wukv_p, qg_p, kg_p = _permute_weights(
        w_in[l], w_uq[l], w_ukv[l], q_head_norm_g[l], k_head_norm_g[l])
    half = QK_ROPE // 2
    inv_freq = ROPE_THETA ** (-jnp.arange(half, dtype=F32) / half)
    invf = jnp.tile(inv_freq, 4).reshape(1, LANES)

    mc, qln, kvln, kpe = _inproj(
        x2, mod3, norm_mix_g[l].reshape(1, D), w_in_p, conv_w[l],
        conv_out_norm_g[l].reshape(1, CONV_DIM), q_lat_norm_g[l].reshape(1, Q_LORA),
        kv_lat_norm_g[l].reshape(1, KV_LORA), S, tm)
    q, k, v = _upproj(qln, kvln, kpe, positions.reshape(T, 1), invf, wuq_p, wukv_p, qg_p, kg_p, B, S, tm)
    mm = _attention(q, k, v, mla_out_norm_g[l], B, S, tq, tq).reshape(T, MLA_DIM)
    x1, h2, top_i, gates = _outproj(
        mc, mm, x2, mod3, w_out[l].astype(BF16), norm_ffn_g[l].reshape(1, D),
        router_w[l].T, router_b[l].reshape(N_EXPERTS, 1), S, tm)

    tile_expert, n_used, buf_tok, dest_orig = _route(top_i, tme)
    y_sorted = _moe(tile_expert, n_used, buf_tok, h2, w_gate_up[l], b_gate_up[l], w_down[l], b_down[l],
                    tme, 256)
    dest_tiles = dest_orig.reshape(T // tmc, tmc, TOP_K).transpose(0, 2, 1)
    out = _combine(dest_tiles, gates.T, x1, mod3, y_sorted, S, tmc)
    return out.reshape(B, S, D)
```

```python
import functools

import jax
import jax.numpy as jnp
from jax import lax
from jax.experimental import pallas as pl
from jax.experimental.pallas import tpu as pltpu

F32 = jnp.float32
BF16 = jnp.bfloat16

D_MODEL = 2048
CONV_GROUPS = 8
CONV_DIM = 1024
CONV_K = 3
MLA_HEADS = 8
QK_NOPE = 128
QK_ROPE = 64
QK_HEAD = QK_NOPE + QK_ROPE
V_HEAD = 128
MLA_DIM = MLA_HEADS * V_HEAD
Q_LORA = 768
KV_LORA = 512
ROPE_THETA = 10000.0
N_EXPERTS = 32
TOP_K = 4
D_EXPERT = D_MODEL
SWIGLU_LIMIT = 7.0
SWIGLU_ALPHA = 1.702
EPS = 1e-6

LANES = 128
QK_PAD = 2 * LANES
V_EXT = 2 * LANES
LOG2E = 1.4426950408889634
_C_B, _C_C, _C_U = 0, CONV_DIM, 2 * CONV_DIM
_C_Q = 3 * CONV_DIM
_C_KV = _C_Q + Q_LORA
_C_PE = _C_KV + KV_LORA
D_IN_PAD = _C_PE + LANES
VMEM_LIMIT = 56 * 1024 * 1024


def _cparams(sem):
    return pltpu.CompilerParams(dimension_semantics=sem, vmem_limit_bytes=VMEM_LIMIT)


def _resident(shape):
    nd = len(shape)
    return pl.BlockSpec(shape, lambda *_: (0,) * nd, pipeline_mode=pl.Buffered(1))


def _split_bf16(a):
    hi = a.astype(BF16)
    lo = (a - hi.astype(F32)).astype(BF16)
    return hi, lo


def _dot(a, b):
    return jnp.dot(a, b, preferred_element_type=F32)


def _dot_nt(a, b):
    return lax.dot_general(a, b, (((1,), (1,)), ((), ())), preferred_element_type=F32)


def _rms(x, n):
    return lax.rsqrt(jnp.sum(x * x, axis=-1, keepdims=True) * (1.0 / n) + EPS)


def _ada_kernel(c_ref, w_ref, b_ref, o_ref):
    c = c_ref[...]
    cond = c * jax.nn.sigmoid(c)
    c_hi, c_lo = _split_bf16(cond)
    w_hi, w_lo = _split_bf16(w_ref[...])
    o_ref[...] = _dot(c_hi, w_hi) + _dot(c_hi, w_lo) + _dot(c_lo, w_hi) + b_ref[...]


def _ada(c, w_ada, b_ada):
    B, D = c.shape
    n_out = w_ada.shape[1]
    rows = 8
    c_pad = jnp.zeros((rows, D), F32).at[:B].set(c)
    tn = 1024
    out = pl.pallas_call(
        _ada_kernel,
        grid=(n_out // tn,),
        in_specs=[pl.BlockSpec((rows, D), lambda j: (0, 0)),
                  pl.BlockSpec((D, tn), lambda j: (0, j)),
                  pl.BlockSpec((1, tn), lambda j: (0, j))],
        out_specs=pl.BlockSpec((rows, tn), lambda j: (0, j)),
        out_shape=jax.ShapeDtypeStruct((rows, n_out), F32),
        compiler_params=_cparams(("arbitrary",)),
        name="ada",
    )(c_pad, w_ada, b_ada.reshape(1, n_out))
    return out[:B]


def _inproj_kernel(x_ref, mod_ref, g_ref, w_ref, cw_ref, cg_ref, qg_ref, kvg_ref,
                   mc_ref, qln_ref, kvln_ref, kpe_ref, cu_scr, *, tm, tiles_per_seq):
    i = pl.program_id(0)
    x = x_ref[...]
    mod = mod_ref[0]
    sh, sc = mod[0:1], mod[1:2]
    h = (x * _rms(x, D_MODEL) * g_ref[...]) * (1.0 + sc) + sh
    hb = h.astype(BF16)

    def proj(lo, hi):
        return _dot(hb, w_ref[:, lo:hi])

    cu = proj(_C_C, _C_U) * proj(_C_U, _C_Q)

    @pl.when(i % tiles_per_seq == 0)
    def _():
        cu_scr[0:8, :] = jnp.zeros((8, CONV_DIM), F32)

    cu_scr[8:8 + tm, :] = cu
    s1 = cu_scr[7:7 + tm, :]
    s2 = cu_scr[6:6 + tm, :]
    cw = cw_ref[...]
    conv = cw[0:1] * s2 + cw[1:2] * s1 + cw[2:3] * cu
    cu_scr[0:8, :] = cu_scr[tm:tm + 8, :]
    y = proj(_C_B, _C_C) * conv
    cg = cg_ref[...]
    for g in range(CONV_GROUPS):
        sl = slice(g * LANES, (g + 1) * LANES)
        yg = y[:, sl]
        mc_ref[:, sl] = (yg * _rms(yg, LANES) * cg[:, sl]).astype(BF16)

    ql = proj(_C_Q, _C_KV)
    qln_ref[...] = (ql * _rms(ql, Q_LORA) * qg_ref[...]).astype(BF16)
    kvl = proj(_C_KV, _C_PE)
    kvln_ref[...] = (kvl * _rms(kvl, KV_LORA) * kvg_ref[...]).astype(BF16)
    kpe_ref[...] = proj(_C_PE, D_IN_PAD)


def _inproj(x2, mod3, norm_g, w_in_p, conv_w, conv_g, q_lat_g, kv_lat_g, S, tm):
    T, D = x2.shape
    tps = S // tm
    row = lambda n: pl.BlockSpec((tm, n), lambda i: (i, 0))
    return pl.pallas_call(
        functools.partial(_inproj_kernel, tm=tm, tiles_per_seq=tps),
        grid=(T // tm,),
        in_specs=[row(D),
                  pl.BlockSpec((1, 6, D), lambda i: (i // tps, 0, 0)),
                  _resident((1, D)),
                  _resident((D, D_IN_PAD)),
                  _resident((CONV_K, CONV_DIM)),
                  _resident((1, CONV_DIM)),
                  _resident((1, Q_LORA)),
                  _resident((1, KV_LORA))],
        out_specs=[row(CONV_DIM), row(Q_LORA), row(KV_LORA), row(LANES)],
        out_shape=[jax.ShapeDtypeStruct((T, CONV_DIM), BF16),
                   jax.ShapeDtypeStruct((T, Q_LORA), BF16),
                   jax.ShapeDtypeStruct((T, KV_LORA), BF16),
                   jax.ShapeDtypeStruct((T, LANES), F32)],
        scratch_shapes=[pltpu.VMEM((tm + 8, CONV_DIM), F32)],
        compiler_params=_cparams(("arbitrary",)),
        name="inproj",
    )(x2, mod3, norm_g, w_in_p, conv_w, conv_g, q_lat_g, kv_lat_g)


def _upproj_kernel(qln_ref, kvln_ref, kpe_ref, pos_ref, invf_ref, wuq_ref, wukv_ref,
                   qg_ref, kg_ref, q_ref, k_ref, v_ref, *, tm):
    ang = pos_ref[...].astype(F32) * invf_ref[...]
    lane = lax.broadcasted_iota(jnp.int32, (tm, LANES), 1)
    cos = jnp.cos(ang)
    sin_signed = jnp.where(lane < LANES // 2, -1.0, 1.0) * jnp.sin(ang)
    even_head = (lane // (QK_ROPE // 2)) % 2 == 0

    def rope(col):
        return col * cos + pltpu.roll(col, LANES // 2, axis=1) * sin_signed

    qg = qg_ref[...]
    kg = kg_ref[...]
    qscale = QK_HEAD ** -0.5 * LOG2E

    q = _dot(qln_ref[...], wuq_ref[...])
    for j in range(MLA_HEADS // 2):
        rc = q[:, MLA_DIM + j * LANES:MLA_DIM + (j + 1) * LANES]
        rc2 = rc * rc
        ss_even = jnp.sum(jnp.where(even_head, rc2, 0.0), axis=-1, keepdims=True)
        ss_odd = jnp.sum(jnp.where(even_head, 0.0, rc2), axis=-1, keepdims=True)
        r = []
        for h, ss_rope in ((2 * j, ss_even), (2 * j + 1, ss_odd)):
            qn = q[:, h * LANES:(h + 1) * LANES]
            ss = jnp.sum(qn * qn, axis=-1, keepdims=True) + ss_rope
            rh = lax.rsqrt(ss * (1.0 / QK_HEAD) + EPS) * qscale
            r.append(rh)
            q_ref[0, h, :, 0:LANES] = (qn * rh * qg[:, 0:LANES]).astype(BF16)
        roped = rope(rc * jnp.where(even_head, r[0], r[1]) * qg[:, LANES:2 * LANES]).astype(BF16)
        q_ref[0, 2 * j, :, LANES:2 * LANES] = roped
        q_ref[0, 2 * j + 1, :, LANES:2 * LANES] = roped

    kv = _dot(kvln_ref[...], wukv_ref[...])
    kp = kpe_ref[...]
    ss_pe = 0.5 * jnp.sum(kp * kp, axis=-1, keepdims=True)
    kr = rope(kp * kg[:, LANES:2 * LANES])
    for h in range(MLA_HEADS):
        kn = kv[:, h * LANES:(h + 1) * LANES]
        ss = jnp.sum(kn * kn, axis=-1, keepdims=True) + ss_pe
        rh = lax.rsqrt(ss * (1.0 / QK_HEAD) + EPS)
        k_ref[0, h, :, 0:LANES] = (kn * rh * kg[:, 0:LANES]).astype(BF16)
        mine = even_head if h % 2 == 0 else jnp.logical_not(even_head)
        k_ref[0, h, :, LANES:2 * LANES] = jnp.where(mine, kr * rh, 0.0).astype(BF16)
        v_ref[0, h, :, 0:V_HEAD] = kv[:, MLA_DIM + h * LANES:MLA_DIM + (h + 1) * LANES].astype(BF16)
        v_ref[0, h, :, V_HEAD:V_EXT] = jnp.ones((tm, V_EXT - V_HEAD), BF16)


def _upproj(qln, kvln, kpe, pos_col, invf, wuq_p, wukv_p, qg_p, kg_p, B, S, tm):
    T = qln.shape[0]
    tps = S // tm
    row = lambda n: pl.BlockSpec((tm, n), lambda i: (i, 0))
    head = lambda n: pl.BlockSpec((1, MLA_HEADS, tm, n), lambda i: (i // tps, 0, i % tps, 0))
    return pl.pallas_call(
        functools.partial(_upproj_kernel, tm=tm),
        grid=(T // tm,),
        in_specs=[row(Q_LORA), row(KV_LORA), row(LANES), row(1),
                  _resident((1, LANES)),
                  _resident(wuq_p.shape), _resident(wukv_p.shape),
                  _resident((1, 2 * LANES)), _resident((1, 2 * LANES))],
        out_specs=[head(QK_PAD), head(QK_PAD), head(V_EXT)],
        out_shape=[jax.ShapeDtypeStruct((B, MLA_HEADS, S, QK_PAD), BF16),
                   jax.ShapeDtypeStruct((B, MLA_HEADS, S, QK_PAD), BF16),
                   jax.ShapeDtypeStruct((B, MLA_HEADS, S, V_EXT), BF16)],
        compiler_params=_cparams(("arbitrary",)),
        name="upproj",
    )(qln, kvln, kpe, pos_col, invf, wuq_p, wukv_p, qg_p, kg_p)


def _attn_kernel(q_ref, k_ref, v_ref, g_ref, o_ref, m_scr, acc_scr, *, ts, nsub, tk):
    qi = pl.program_id(1)
    tq = ts * nsub
    for a in range(nsub):
        m_scr[a] = jnp.full((ts, LANES), -jnp.inf, F32)
        acc_scr[a] = jnp.zeros((ts, V_EXT), F32)

    def step(a, start, width, masked):
        q = q_ref[0, a * ts:(a + 1) * ts, :]
        k = k_ref[0, pl.ds(start, width), :]
        v = v_ref[0, pl.ds(start, width), :]
        s = _dot_nt(q, k)
        if masked:
            row = lax.broadcasted_iota(jnp.int32, (ts, width), 0) + (qi * tq + a * ts)
            col = lax.broadcasted_iota(jnp.int32, (ts, width), 1) + start
            s = jnp.where(col <= row, s, -jnp.inf)
        m_prev = m_scr[a]
        m_new = jnp.maximum(m_prev, jnp.max(s, axis=-1, keepdims=True))
        p = jnp.exp2(s - jnp.concatenate([m_new] * (width // LANES), axis=1))
        corr = jnp.exp2(m_prev - m_new)
        acc_scr[a] = (jnp.concatenate([corr] * (V_EXT // LANES), axis=1) * acc_scr[a]
                      + _dot(p.astype(BF16), v))
        m_scr[a] = m_new

    def body(j, carry):
        start = pl.multiple_of(j * tk, tk)
        for a in range(nsub):
            step(a, start, tk, False)
        return carry

    lax.fori_loop(0, qi * (tq // tk), body, 0)
    for a in range(nsub):
        for d in range(a + 1):
            step(a, pl.multiple_of(qi * tq + d * ts, ts), ts, d == a)

    for a in range(nsub):
        acc = acc_scr[a]
        o = acc[:, 0:V_HEAD] / acc[:, V_HEAD:V_EXT]
        o_ref[0, a * ts:(a + 1) * ts, :] = (o * _rms(o, V_HEAD) * g_ref[0]).astype(BF16)


def _attention(q, k, v, mla_g, B, S, ts, nsub, tk):
    BH = B * MLA_HEADS
    H = MLA_HEADS
    tq = ts * nsub
    return pl.pallas_call(
        functools.partial(_attn_kernel, ts=ts, nsub=nsub, tk=tk),
        grid=(BH, S // tq),
        in_specs=[pl.BlockSpec((1, tq, QK_PAD), lambda bh, qi: (bh, qi, 0)),
                  pl.BlockSpec((1, S, QK_PAD), lambda bh, qi: (bh, 0, 0)),
                  pl.BlockSpec((1, S, V_EXT), lambda bh, qi: (bh, 0, 0)),
                  pl.BlockSpec((1, 1, V_HEAD), lambda bh, qi: (bh % H, 0, 0))],
        out_specs=pl.BlockSpec((1, tq, V_HEAD), lambda bh, qi: (bh // H, qi, bh % H)),
        out_shape=jax.ShapeDtypeStruct((B, S, MLA_DIM), BF16),
        scratch_shapes=[pltpu.VMEM((nsub, ts, LANES), F32), pltpu.VMEM((nsub, ts, V_EXT), F32)],
        compiler_params=_cparams(("arbitrary", "arbitrary")),
        name="attn",
    )(q.reshape(BH, S, QK_PAD), k.reshape(BH, S, QK_PAD), v.reshape(BH, S, V_EXT),
      mla_g.reshape(H, 1, V_HEAD))


def _outproj_kernel(mc_ref, mm_ref, x_ref, mod_ref, w_ref, g_ref, rw_ref, rb_ref,
                    x1_ref, h2_ref, idx_ref, gate_ref, *, tm):
    mod = mod_ref[0]
    g_m, sh_f, sc_f = mod[2:3], mod[3:4], mod[4:5]
    mix = _dot(mc_ref[...], w_ref[0:CONV_DIM, :]) + _dot(mm_ref[...], w_ref[CONV_DIM:, :])
    x1 = x_ref[...] + g_m * mix
    x1_ref[...] = x1
    h2 = (x1 * _rms(x1, D_MODEL) * g_ref[...]) * (1.0 + sc_f) + sh_f
    h2_ref[...] = h2

    h_hi, h_lo = _split_bf16(h2)
    r_hi, r_lo = _split_bf16(rw_ref[...])
    logits = _dot_nt(r_hi, h_hi) + _dot_nt(r_lo, h_hi) + _dot_nt(r_hi, h_lo) + rb_ref[...]
    eid = lax.broadcasted_iota(jnp.int32, (N_EXPERTS, tm), 0)
    vals = []
    for kk in range(TOP_K):
        mx = jnp.max(logits, axis=0, keepdims=True)
        idx = jnp.min(jnp.where(logits == mx, eid, N_EXPERTS), axis=0, keepdims=True)
        idx_ref[kk:kk + 1, :] = idx
        vals.append(mx)
        logits = jnp.where(eid == idx, -jnp.inf, logits)
    ex = [jnp.exp(vv - vals[0]) for vv in vals]
    denom = ex[0] + ex[1] + ex[2] + ex[3]
    for kk in range(TOP_K):
        gate_ref[kk:kk + 1, :] = ex[kk] / denom


def _outproj(mc, mm, x2, mod3, w_out_b, norm_g, rw_t, rb_col, S, tm):
    T, D = x2.shape
    tps = S // tm
    row = lambda n: pl.BlockSpec((tm, n), lambda i: (i, 0))
    colblk = lambda n: pl.BlockSpec((n, tm), lambda i: (0, i))
    return pl.pallas_call(
        functools.partial(_outproj_kernel, tm=tm),
        grid=(T // tm,),
        in_specs=[row(CONV_DIM), row(MLA_DIM), row(D),
                  pl.BlockSpec((1, 6, D), lambda i: (i // tps, 0, 0)),
                  _resident((D, D)), _resident((1, D)),
                  _resident((N_EXPERTS, D)), _resident((N_EXPERTS, 1))],
        out_specs=[row(D), row(D), colblk(TOP_K), colblk(TOP_K)],
        out_shape=[jax.ShapeDtypeStruct((T, D), F32),
                   jax.ShapeDtypeStruct((T, D), F32),
                   jax.ShapeDtypeStruct((TOP_K, T), jnp.int32),
                   jax.ShapeDtypeStruct((TOP_K, T), F32)],
        compiler_params=_cparams(("arbitrary",)),
        name="outproj",
    )(mc, mm, x2, mod3, w_out_b, norm_g, rw_t, rb_col)


def _row_copy(src_hbm, src_row, dst, sem):
    return pltpu.make_async_copy(src_hbm.at[pl.ds(src_row, 1)], dst, sem)


def _start_rows(idx_at, n_rows, src_hbm, dst_at, sem):
    def issue(r, c):
        _row_copy(src_hbm, idx_at(r), dst_at(r), sem).start()
        return c

    lax.fori_loop(0, n_rows, issue, 0)


def _wait_rows(n_rows, src_hbm, dst_at, sem):
    def wait(r, c):
        _row_copy(src_hbm, 0, dst_at(r), sem).wait()
        return c

    lax.fori_loop(0, n_rows, wait, 0)


def _moe_kernel(te_ref, nu_ref, tok_ref, tok_next_ref, h_hbm, wg_ref, wu_ref, bg_ref, bu_ref, wd_ref, bd_ref,
                o_ref, xbuf, x_bf, act, sem, *, tme, nf):
    i = pl.program_id(0)
    f = pl.program_id(1)
    n_used = nu_ref[0]
    used = i < n_used
    xrow = lambda r: xbuf.at[pl.ds(r, 1)]

    @pl.when(jnp.logical_and(i == 0, f == 0))
    def _():
        _start_rows(lambda r: tok_ref[0, 0, r], tme, h_hbm, xrow, sem)

    @pl.when(jnp.logical_and(used, f == 0))
    def _():
        _wait_rows(tme, h_hbm, xrow, sem)
        x_bf[...] = xbuf[...].astype(BF16)

    @pl.when(jnp.logical_and(i + 1 < n_used, f == 1))
    def _():
        _start_rows(lambda r: tok_next_ref[0, 0, r], tme, h_hbm, xrow, sem)

    @pl.when(jnp.logical_and(used, f < nf))
    def _():
        x = x_bf[...]
        g = _dot(x, wg_ref[0].astype(BF16)) + bg_ref[0]
        u = _dot(x, wu_ref[0].astype(BF16)) + bu_ref[0]
        g = jnp.minimum(g, SWIGLU_LIMIT)
        u = jnp.clip(u, -SWIGLU_LIMIT, SWIGLU_LIMIT)
        act[f] = ((u + 1.0) * g * jax.nn.sigmoid(SWIGLU_ALPHA * g)).astype(BF16)

    @pl.when(jnp.logical_and(used, f >= nf))
    def _():
        a = jnp.concatenate([act[c] for c in range(nf)], axis=1)
        o_ref[...] = _dot(a, wd_ref[0].astype(BF16)) + bd_ref[0]

    @pl.when(jnp.logical_and(jnp.logical_not(used), f >= nf))
    def _():
        o_ref[...] = jnp.zeros(o_ref.shape, F32)


def _moe(tile_expert, n_used, buf_tok, h2, w_gu, b_gu, w_d, b_d, tme, tf):
    n_tiles = tile_expert.shape[0]
    D = h2.shape[1]
    nf = D_EXPERT // tf
    assert nf >= 2 and D % tf == 0

    def up(i, f, nu):
        return jnp.where(i < nu[0], jnp.minimum(f, nf - 1), nf - 1)

    def down(f):
        return jnp.maximum(f - nf, 0)

    def down_w(i, f, nu):
        return jnp.where(i < nu[0], down(f), nf - 1)

    tok3 = buf_tok.reshape(n_tiles, 1, tme)
    grid_spec = pltpu.PrefetchScalarGridSpec(
        num_scalar_prefetch=2,
        grid=(n_tiles, 2 * nf),
        in_specs=[
            pl.BlockSpec((1, 1, tme), lambda i, f, te, nu: (i, 0, 0), memory_space=pltpu.SMEM),
            pl.BlockSpec((1, 1, tme), lambda i, f, te, nu: (jnp.minimum(i + 1, n_tiles - 1), 0, 0),
                         memory_space=pltpu.SMEM),
            pl.BlockSpec(memory_space=pl.ANY),
            pl.BlockSpec((1, D, tf), lambda i, f, te, nu: (te[i], 0, up(i, f, nu))),
            pl.BlockSpec((1, D, tf), lambda i, f, te, nu: (te[i], 0, nf + up(i, f, nu))),
            pl.BlockSpec((1, 1, tf), lambda i, f, te, nu: (te[i], 0, up(i, f, nu))),
            pl.BlockSpec((1, 1, tf), lambda i, f, te, nu: (te[i], 0, nf + up(i, f, nu))),
            pl.BlockSpec((1, D_EXPERT, tf), lambda i, f, te, nu: (te[i], 0, down_w(i, f, nu))),
            pl.BlockSpec((1, 1, tf), lambda i, f, te, nu: (te[i], 0, down_w(i, f, nu))),
        ],
        out_specs=pl.BlockSpec((tme, tf), lambda i, f, te, nu: (i, down(f))),
        scratch_shapes=[pltpu.VMEM((tme, D), F32), pltpu.VMEM((tme, D), BF16),
                        pltpu.VMEM((nf, tme, tf), BF16), pltpu.SemaphoreType.DMA(())],
    )
    return pl.pallas_call(
        functools.partial(_moe_kernel, tme=tme, nf=nf),
        grid_spec=grid_spec,
        out_shape=jax.ShapeDtypeStruct((n_tiles * tme, D), F32),
        compiler_params=_cparams(("arbitrary", "arbitrary")),
        name="moe",
    )(tile_expert, n_used, tok3, tok3, h2,
      w_gu, w_gu, b_gu.reshape(N_EXPERTS, 1, 2 * D_EXPERT), b_gu.reshape(N_EXPERTS, 1, 2 * D_EXPERT),
      w_d, b_d.reshape(N_EXPERTS, 1, D))


def _combine_kernel(dest_ref, dest_next_ref, gate_ref, x1_ref, mod_ref, y_hbm, o_ref, ybuf, sem, *, tm, n_tiles):
    i = pl.program_id(0)
    slot = i % 2

    def start(dref, sl):
        for kk in range(TOP_K):
            _start_rows(lambda r, kk=kk: dref[0, kk, r], tm, y_hbm,
                        lambda r, kk=kk: ybuf.at[sl, kk, pl.ds(r, 1)], sem.at[sl])

    @pl.when(i == 0)
    def _():
        start(dest_ref, 0)

    @pl.when(i + 1 < n_tiles)
    def _():
        start(dest_next_ref, 1 - slot)

    for kk in range(TOP_K):
        _wait_rows(tm, y_hbm, lambda r, kk=kk: ybuf.at[slot, kk, pl.ds(r, 1)], sem.at[slot])
    gates = gate_ref[...]
    y = gates[:, 0:1] * ybuf[slot, 0]
    for kk in range(1, TOP_K):
        y = y + gates[:, kk:kk + 1] * ybuf[slot, kk]
    g_f = mod_ref[0][5:6]
    o_ref[...] = x1_ref[...] + g_f * y


def _combine(dest_tiles, gates_t, x1, mod3, y_sorted, S, tm):
    T, D = x1.shape
    tps = S // tm
    n_tiles = T // tm
    return pl.pallas_call(
        functools.partial(_combine_kernel, tm=tm, n_tiles=n_tiles),
        grid=(n_tiles,),
        in_specs=[pl.BlockSpec((1, TOP_K, tm), lambda i: (i, 0, 0), memory_space=pltpu.SMEM),
                  pl.BlockSpec((1, TOP_K, tm), lambda i: (jnp.minimum(i + 1, n_tiles - 1), 0, 0),
                               memory_space=pltpu.SMEM),
                  pl.BlockSpec((tm, TOP_K), lambda i: (i, 0)),
                  pl.BlockSpec((tm, D), lambda i: (i, 0)),
                  pl.BlockSpec((1, 6, D), lambda i: (i // tps, 0, 0)),
                  pl.BlockSpec(memory_space=pl.ANY)],
        out_specs=pl.BlockSpec((tm, D), lambda i: (i, 0)),
        out_shape=jax.ShapeDtypeStruct((T, D), F32),
        scratch_shapes=[pltpu.VMEM((2, TOP_K, tm, D), F32), pltpu.SemaphoreType.DMA((2,))],
        compiler_params=_cparams(("arbitrary",)),
        name="combine",
    )(dest_tiles, dest_tiles, gates_t, x1, mod3, y_sorted)


def _route(top_i, tme):
    T = top_i.shape[1]
    TK = T * TOP_K
    flat_e = top_i.T.reshape(-1)
    onehot = (flat_e[:, None] == jnp.arange(N_EXPERTS, dtype=jnp.int32)[None, :]).astype(jnp.int32)
    csum = jnp.cumsum(onehot, axis=0)
    counts = csum[-1]
    padded = ((counts + tme - 1) // tme) * tme
    padded_end = jnp.cumsum(padded)
    padded_start = padded_end - padded
    group_start = jnp.cumsum(counts) - counts
    dest_orig = jnp.sum(onehot * (padded_start[None, :] + csum - onehot), axis=1).astype(jnp.int32)

    n_tiles = -(-TK // tme) + N_EXPERTS
    tile_start = jnp.arange(n_tiles) * tme
    tile_expert = jnp.minimum(jnp.searchsorted(padded_end, tile_start, side='right'),
                              N_EXPERTS - 1).astype(jnp.int32)
    n_used = (padded_end[-1] // tme).astype(jnp.int32).reshape(1)

    tok_sorted = (jnp.argsort(flat_e) // TOP_K).astype(jnp.int32)
    r = (tile_start - padded_start[tile_expert])[:, None] + jnp.arange(tme)[None, :]
    valid = r < counts[tile_expert][:, None]
    src = jnp.clip(group_start[tile_expert][:, None] + r, 0, TK - 1)
    buf_tok = jnp.where(valid, tok_sorted[src.reshape(-1)].reshape(n_tiles, tme), 0).reshape(-1)
    return tile_expert, n_used, buf_tok, dest_orig


def _permute_weights(w_in, w_uq, w_ukv, q_head_g, k_head_g):
    half = QK_ROPE // 2
    pe0 = _C_PE
    pe_cols = jnp.concatenate([jnp.arange(pe0, pe0 + half), jnp.arange(pe0, pe0 + half),
                               jnp.arange(pe0 + half, pe0 + 2 * half), jnp.arange(pe0 + half, pe0 + 2 * half)])
    w_in_p = jnp.concatenate([w_in[:, :pe0], w_in[:, pe_cols]], axis=1).astype(BF16)

    hq = jnp.arange(MLA_HEADS)[:, None] * QK_HEAD
    nope_cols = (hq + jnp.arange(QK_NOPE)[None, :]).reshape(-1)
    rope_cols = []
    for j in range(MLA_HEADS // 2):
        for part in range(2):
            for h in (2 * j, 2 * j + 1):
                rope_cols.append(h * QK_HEAD + QK_NOPE + part * half + jnp.arange(half))
    wuq_p = w_uq[:, jnp.concatenate([nope_cols] + rope_cols)].astype(BF16)

    hk = jnp.arange(MLA_HEADS)[:, None] * (QK_NOPE + V_HEAD)
    kn_cols = (hk + jnp.arange(QK_NOPE)[None, :]).reshape(-1)
    v_cols = (hk + QK_NOPE + jnp.arange(V_HEAD)[None, :]).reshape(-1)
    wukv_p = w_ukv[:, jnp.concatenate([kn_cols, v_cols])].astype(BF16)

    def gain_p(g):
        x1, x2 = g[QK_NOPE:QK_NOPE + half], g[QK_NOPE + half:]
        return jnp.concatenate([g[:QK_NOPE], x1, x1, x2, x2]).reshape(1, 2 * LANES)

    return w_in_p, wuq_p, wukv_p, gain_p(q_head_g), gain_p(k_head_g)


def _tile(n, pref):
    t = pref
    while n % t:
        t //= 2
    return t


def kernel(x, c, positions, w_ada, b_ada, norm_mix_g, w_in, conv_w, q_lat_norm_g, w_uq, kv_lat_norm_g, w_ukv, q_head_norm_g, k_head_norm_g, conv_out_norm_g, mla_out_norm_g, w_out, norm_ffn_g, router_w, router_b, w_gate_up, b_gate_up, w_down, b_down):
    B, S, D = x.shape
    T = B * S
    assert D == D_MODEL and w_ada.shape[0] == 1 and S % 128 == 0
    l = 0
    tm = _tile(S, 512)
    ts = 512 if S >= 8192 else S // 8
    tme = 1024 if T * TOP_K >= 32768 else 256
    tmc = _tile(S, 256)

    mod3 = _ada(c, w_ada[l], b_ada[l]).reshape(B, 6, D)
    x2 = x.reshape(T, D)

    w_in_p, wuq_p, wukv_p, qg_p, kg_p = _permute_weights(
        w_in[l], w_uq[l], w_ukv[l], q_head_norm_g[l], k_head_norm_g[l])
    half = QK_ROPE // 2
    inv_freq = ROPE_THETA ** (-jnp.arange(half, dtype=F32) / half)
    invf = jnp.tile(inv_freq, 4).reshape(1, LANES)

    mc, qln, kvln, kpe = _inproj(
        x2, mod3, norm_mix_g[l].reshape(1, D), w_in_p, conv_w[l],
        conv_out_norm_g[l].reshape(1, CONV_DIM), q_lat_norm_g[l].reshape(1, Q_LORA),
        kv_lat_norm_g[l].reshape(1, KV_LORA), S, tm)
    q, k, v = _upproj(qln, kvln, kpe, positions.reshape(T, 1), invf, wuq_p, wukv_p, qg_p, kg_p, B, S, tm)
    mm = _attention(q, k, v, mla_out_norm_g[l], B, S, ts, 4, 2 * ts).reshape(T, MLA_DIM)
    x1, h2, top_i, gates = _outproj(
        mc, mm, x2, mod3, w_out[l].astype(BF16), norm_ffn_g[l].reshape(1, D),
        router_w[l].T, router_b[l].reshape(N_EXPERTS, 1), S, tm)

    tile_expert, n_used, buf_tok, dest_orig = _route(top_i, tme)
    y_sorted = _moe(tile_expert, n_used, buf_tok, h2, w_gate_up[l], b_gate_up[l], w_down[l], b_down[l],
                    tme, 512)
    dest_tiles = dest_orig.reshape(T // tmc, tmc, TOP_K).transpose(0, 2, 1)
    out = _combine(dest_tiles, gates.T, x1, mod3, y_sorted, S, tmc)
    return out.reshape(B, S, D)
```

```python
import functools

import jax
import jax.numpy as jnp
from jax import lax
from jax.experimental import pallas as pl
from jax.experimental.pallas import tpu as pltpu

F32 = jnp.float32
BF16 = jnp.bfloat16

D_MODEL = 2048
CONV_GROUPS = 8
CONV_DIM = 1024
CONV_K = 3
MLA_HEADS = 8
QK_NOPE = 128
QK_ROPE = 64
QK_HEAD = QK_NOPE + QK_ROPE
V_HEAD = 128
MLA_DIM = MLA_HEADS * V_HEAD
Q_LORA = 768
KV_LORA = 512
ROPE_THETA = 10000.0
N_EXPERTS = 32
TOP_K = 4
D_EXPERT = D_MODEL
SWIGLU_LIMIT = 7.0
SWIGLU_ALPHA = 1.702
EPS = 1e-6

LANES = 128
QK_PAD = 2 * LANES
V_EXT = 2 * LANES
LOG2E = 1.4426950408889634
_C_B, _C_C, _C_U = 0, CONV_DIM, 2 * CONV_DIM
_C_Q = 3 * CONV_DIM
_C_KV = _C_Q + Q_LORA
_C_PE = _C_KV + KV_LORA
D_IN_PAD = _C_PE + LANES
VMEM_LIMIT = 56 * 1024 * 1024


def _cparams(sem):
    return pltpu.CompilerParams(dimension_semantics=sem, vmem_limit_bytes=VMEM_LIMIT)


def _resident(shape):
    nd = len(shape)
    return pl.BlockSpec(shape, lambda *_: (0,) * nd, pipeline_mode=pl.Buffered(1))


def _split_bf16(a):
    hi = a.astype(BF16)
    lo = (a - hi.astype(F32)).astype(BF16)
    return hi, lo


def _dot(a, b):
    return jnp.dot(a, b, preferred_element_type=F32)


def _dot_nt(a, b):
    return lax.dot_general(a, b, (((1,), (1,)), ((), ())), preferred_element_type=F32)


def _rms(x, n):
    return lax.rsqrt(jnp.sum(x * x, axis=-1, keepdims=True) * (1.0 / n) + EPS)


def _ada_kernel(c_ref, w_ref, b_ref, o_ref):
    c = c_ref[...]
    cond = c * jax.nn.sigmoid(c)
    c_hi, c_lo = _split_bf16(cond)
    w_hi, w_lo = _split_bf16(w_ref[...])
    o_ref[...] = _dot(c_hi, w_hi) + _dot(c_hi, w_lo) + _dot(c_lo, w_hi) + b_ref[...]


def _ada(c, w_ada, b_ada):
    B, D = c.shape
    n_out = w_ada.shape[1]
    rows = 8
    c_pad = jnp.zeros((rows, D), F32).at[:B].set(c)
    tn = 1024
    out = pl.pallas_call(
        _ada_kernel,
        grid=(n_out // tn,),
        in_specs=[pl.BlockSpec((rows, D), lambda j: (0, 0)),
                  pl.BlockSpec((D, tn), lambda j: (0, j)),
                  pl.BlockSpec((1, tn), lambda j: (0, j))],
        out_specs=pl.BlockSpec((rows, tn), lambda j: (0, j)),
        out_shape=jax.ShapeDtypeStruct((rows, n_out), F32),
        compiler_params=_cparams(("arbitrary",)),
        name="ada",
    )(c_pad, w_ada, b_ada.reshape(1, n_out))
    return out[:B]


def _inproj_kernel(x_ref, mod_ref, g_ref, w_ref, cw_ref, cg_ref, qg_ref, kvg_ref,
                   mc_ref, qln_ref, kvln_ref, kpe_ref, cu_scr, *, tm, tiles_per_seq):
    i = pl.program_id(0)
    x = x_ref[...]
    mod = mod_ref[0]
    sh, sc = mod[0:1], mod[1:2]
    h = (x * _rms(x, D_MODEL) * g_ref[...]) * (1.0 + sc) + sh
    hb = h.astype(BF16)

    def proj(lo, hi):
        return _dot(hb, w_ref[:, lo:hi])

    cu = proj(_C_C, _C_U) * proj(_C_U, _C_Q)

    @pl.when(i % tiles_per_seq == 0)
    def _():
        cu_scr[0:8, :] = jnp.zeros((8, CONV_DIM), F32)

    cu_scr[8:8 + tm, :] = cu
    s1 = cu_scr[7:7 + tm, :]
    s2 = cu_scr[6:6 + tm, :]
    cw = cw_ref[...]
    conv = cw[0:1] * s2 + cw[1:2] * s1 + cw[2:3] * cu
    cu_scr[0:8, :] = cu_scr[tm:tm + 8, :]
    y = proj(_C_B, _C_C) * conv
    cg = cg_ref[...]
    for g in range(CONV_GROUPS):
        sl = slice(g * LANES, (g + 1) * LANES)
        yg = y[:, sl]
        mc_ref[:, sl] = (yg * _rms(yg, LANES) * cg[:, sl]).astype(BF16)

    ql = proj(_C_Q, _C_KV)
    qln_ref[...] = (ql * _rms(ql, Q_LORA) * qg_ref[...]).astype(BF16)
    kvl = proj(_C_KV, _C_PE)
    kvln_ref[...] = (kvl * _rms(kvl, KV_LORA) * kvg_ref[...]).astype(BF16)
    kpe_ref[...] = proj(_C_PE, D_IN_PAD)


def _inproj(x2, mod3, norm_g, w_in_p, conv_w, conv_g, q_lat_g, kv_lat_g, S, tm):
    T, D = x2.shape
    tps = S // tm
    row = lambda n: pl.BlockSpec((tm, n), lambda i: (i, 0))
    return pl.pallas_call(
        functools.partial(_inproj_kernel, tm=tm, tiles_per_seq=tps),
        grid=(T // tm,),
        in_specs=[row(D),
                  pl.BlockSpec((1, 6, D), lambda i: (i // tps, 0, 0)),
                  _resident((1, D)),
                  _resident((D, D_IN_PAD)),
                  _resident((CONV_K, CONV_DIM)),
                  _resident((1, CONV_DIM)),
                  _resident((1, Q_LORA)),
                  _resident((1, KV_LORA))],
        out_specs=[row(CONV_DIM), row(Q_LORA), row(KV_LORA), row(LANES)],
        out_shape=[jax.ShapeDtypeStruct((T, CONV_DIM), BF16),
                   jax.ShapeDtypeStruct((T, Q_LORA), BF16),
                   jax.ShapeDtypeStruct((T, KV_LORA), BF16),
                   jax.ShapeDtypeStruct((T, LANES), F32)],
        scratch_shapes=[pltpu.VMEM((tm + 8, CONV_DIM), F32)],
        compiler_params=_cparams(("arbitrary",)),
        name="inproj",
    )(x2, mod3, norm_g, w_in_p, conv_w, conv_g, q_lat_g, kv_lat_g)


def _upproj_kernel(qln_ref, kvln_ref, kpe_ref, pos_ref, invf_ref, wuq_ref, wukv_ref,
                   qg_ref, kg_ref, q_ref, k_ref, v_ref, *, tm):
    ang = pos_ref[...].astype(F32) * invf_ref[...]
    lane = lax.broadcasted_iota(jnp.int32, (tm, LANES), 1)
    cos = jnp.cos(ang)
    sin_signed = jnp.where(lane < LANES // 2, -1.0, 1.0) * jnp.sin(ang)
    even_head = (lane // (QK_ROPE // 2)) % 2 == 0

    def rope(col):
        return col * cos + pltpu.roll(col, LANES // 2, axis=1) * sin_signed

    qg = qg_ref[...]
    kg = kg_ref[...]
    qscale = QK_HEAD ** -0.5 * LOG2E

    q = _dot(qln_ref[...], wuq_ref[...])
    for j in range(MLA_HEADS // 2):
        rc = q[:, MLA_DIM + j * LANES:MLA_DIM + (j + 1) * LANES]
        rc2 = rc * rc
        ss_even = jnp.sum(jnp.where(even_head, rc2, 0.0), axis=-1, keepdims=True)
        ss_odd = jnp.sum(jnp.where(even_head, 0.0, rc2), axis=-1, keepdims=True)
        r = []
        for h, ss_rope in ((2 * j, ss_even), (2 * j + 1, ss_odd)):
            qn = q[:, h * LANES:(h + 1) * LANES]
            ss = jnp.sum(qn * qn, axis=-1, keepdims=True) + ss_rope
            rh = lax.rsqrt(ss * (1.0 / QK_HEAD) + EPS) * qscale
            r.append(rh)
            q_ref[0, h, :, 0:LANES] = (qn * rh * qg[:, 0:LANES]).astype(BF16)
        roped = rope(rc * jnp.where(even_head, r[0], r[1]) * qg[:, LANES:2 * LANES]).astype(BF16)
        q_ref[0, 2 * j, :, LANES:2 * LANES] = roped
        q_ref[0, 2 * j + 1, :, LANES:2 * LANES] = roped

    kv = _dot(kvln_ref[...], wukv_ref[...])
    kp = kpe_ref[...]
    ss_pe = 0.5 * jnp.sum(kp * kp, axis=-1, keepdims=True)
    kr = rope(kp * kg[:, LANES:2 * LANES])
    for h in range(MLA_HEADS):
        kn = kv[:, h * LANES:(h + 1) * LANES]
        ss = jnp.sum(kn * kn, axis=-1, keepdims=True) + ss_pe
        rh = lax.rsqrt(ss * (1.0 / QK_HEAD) + EPS)
        k_ref[0, h, :, 0:LANES] = (kn * rh * kg[:, 0:LANES]).astype(BF16)
        mine = even_head if h % 2 == 0 else jnp.logical_not(even_head)
        k_ref[0, h, :, LANES:2 * LANES] = jnp.where(mine, kr * rh, 0.0).astype(BF16)
        v_ref[0, h, :, 0:V_HEAD] = kv[:, MLA_DIM + h * LANES:MLA_DIM + (h + 1) * LANES].astype(BF16)
        v_ref[0, h, :, V_HEAD:V_EXT] = jnp.ones((tm, V_EXT - V_HEAD), BF16)


def _upproj(qln, kvln, kpe, pos_col, invf, wuq_p, wukv_p, qg_p, kg_p, B, S, tm):
    T = qln.shape[0]
    tps = S // tm
    row = lambda n: pl.BlockSpec((tm, n), lambda i: (i, 0))
    head = lambda n: pl.BlockSpec((1, MLA_HEADS, tm, n), lambda i: (i // tps, 0, i % tps, 0))
    return pl.pallas_call(
        functools.partial(_upproj_kernel, tm=tm),
        grid=(T // tm,),
        in_specs=[row(Q_LORA), row(KV_LORA), row(LANES), row(1),
                  _resident((1, LANES)),
                  _resident(wuq_p.shape), _resident(wukv_p.shape),
                  _resident((1, 2 * LANES)), _resident((1, 2 * LANES))],
        out_specs=[head(QK_PAD), head(QK_PAD), head(V_EXT)],
        out_shape=[jax.ShapeDtypeStruct((B, MLA_HEADS, S, QK_PAD), BF16),
                   jax.ShapeDtypeStruct((B, MLA_HEADS, S, QK_PAD), BF16),
                   jax.ShapeDtypeStruct((B, MLA_HEADS, S, V_EXT), BF16)],
        compiler_params=_cparams(("arbitrary",)),
        name="upproj",
    )(qln, kvln, kpe, pos_col, invf, wuq_p, wukv_p, qg_p, kg_p)


def _attn_kernel(q_ref, k_ref, v_ref, g_ref, o_ref, m_scr, acc_scr, *, ts, nsub, tk):
    qi = pl.program_id(1)
    tq = ts * nsub
    for a in range(nsub):
        m_scr[a] = jnp.full((ts, LANES), -jnp.inf, F32)
        acc_scr[a] = jnp.zeros((ts, V_EXT), F32)

    def step(a, start, width, masked):
        q = q_ref[0, a * ts:(a + 1) * ts, :]
        k = k_ref[0, pl.ds(start, width), :]
        v = v_ref[0, pl.ds(start, width), :]
        s = _dot_nt(q, k)
        if masked:
            row = lax.broadcasted_iota(jnp.int32, (ts, width), 0) + (qi * tq + a * ts)
            col = lax.broadcasted_iota(jnp.int32, (ts, width), 1) + start
            s = jnp.where(col <= row, s, -jnp.inf)
        m_prev = m_scr[a]
        m_new = jnp.maximum(m_prev, jnp.max(s, axis=-1, keepdims=True))
        p = jnp.exp2(s - jnp.concatenate([m_new] * (width // LANES), axis=1))
        corr = jnp.exp2(m_prev - m_new)
        acc_scr[a] = (jnp.concatenate([corr] * (V_EXT // LANES), axis=1) * acc_scr[a]
                      + _dot(p.astype(BF16), v))
        m_scr[a] = m_new

    def body(j, carry):
        start = pl.multiple_of(j * tk, tk)
        for a in range(nsub):
            step(a, start, tk, False)
        return carry

    lax.fori_loop(0, qi * (tq // tk), body, 0)
    for a in range(nsub):
        for d in range(a + 1):
            step(a, pl.multiple_of(qi * tq + d * ts, ts), ts, d == a)

    for a in range(nsub):
        acc = acc_scr[a]
        o = acc[:, 0:V_HEAD] / acc[:, V_HEAD:V_EXT]
        o_ref[0, a * ts:(a + 1) * ts, :] = (o * _rms(o, V_HEAD) * g_ref[0]).astype(BF16)


def _attention(q, k, v, mla_g, B, S, ts, nsub, tk):
    BH = B * MLA_HEADS
    H = MLA_HEADS
    tq = ts * nsub
    return pl.pallas_call(
        functools.partial(_attn_kernel, ts=ts, nsub=nsub, tk=tk),
        grid=(BH, S // tq),
        in_specs=[pl.BlockSpec((1, tq, QK_PAD), lambda bh, qi: (bh, qi, 0)),
                  pl.BlockSpec((1, S, QK_PAD), lambda bh, qi: (bh, 0, 0)),
                  pl.BlockSpec((1, S, V_EXT), lambda bh, qi: (bh, 0, 0)),
                  pl.BlockSpec((1, 1, V_HEAD), lambda bh, qi: (bh % H, 0, 0))],
        out_specs=pl.BlockSpec((1, tq, V_HEAD), lambda bh, qi: (bh // H, qi, bh % H)),
        out_shape=jax.ShapeDtypeStruct((B, S, MLA_DIM), BF16),
        scratch_shapes=[pltpu.VMEM((nsub, ts, LANES), F32), pltpu.VMEM((nsub, ts, V_EXT), F32)],
        compiler_params=_cparams(("arbitrary", "arbitrary")),
        name="attn",
    )(q.reshape(BH, S, QK_PAD), k.reshape(BH, S, QK_PAD), v.reshape(BH, S, V_EXT),
      mla_g.reshape(H, 1, V_HEAD))


def _outproj_kernel(mc_ref, mm_ref, x_ref, mod_ref, w_ref, g_ref, rw_ref, rb_ref,
                    x1_ref, h2_ref, idx_ref, gate_ref, *, tm):
    mod = mod_ref[0]
    g_m, sh_f, sc_f = mod[2:3], mod[3:4], mod[4:5]
    mix = _dot(mc_ref[...], w_ref[0:CONV_DIM, :]) + _dot(mm_ref[...], w_ref[CONV_DIM:, :])
    x1 = x_ref[...] + g_m * mix
    x1_ref[...] = x1
    h2 = (x1 * _rms(x1, D_MODEL) * g_ref[...]) * (1.0 + sc_f) + sh_f
    h2_ref[...] = h2

    h_hi, h_lo = _split_bf16(h2)
    r_hi, r_lo = _split_bf16(rw_ref[...])
    logits = _dot_nt(r_hi, h_hi) + _dot_nt(r_lo, h_hi) + _dot_nt(r_hi, h_lo) + rb_ref[...]
    eid = lax.broadcasted_iota(jnp.int32, (N_EXPERTS, tm), 0)
    vals = []
    for kk in range(TOP_K):
        mx = jnp.max(logits, axis=0, keepdims=True)
        idx = jnp.min(jnp.where(logits == mx, eid, N_EXPERTS), axis=0, keepdims=True)
        idx_ref[kk:kk + 1, :] = idx
        vals.append(mx)
        logits = jnp.where(eid == idx, -jnp.inf, logits)
    ex = [jnp.exp(vv - vals[0]) for vv in vals]
    denom = ex[0] + ex[1] + ex[2] + ex[3]
    for kk in range(TOP_K):
        gate_ref[kk:kk + 1, :] = ex[kk] / denom


def _outproj(mc, mm, x2, mod3, w_out_b, norm_g, rw_t, rb_col, S, tm):
    T, D = x2.shape
    tps = S // tm
    row = lambda n: pl.BlockSpec((tm, n), lambda i: (i, 0))
    colblk = lambda n: pl.BlockSpec((n, tm), lambda i: (0, i))
    return pl.pallas_call(
        functools.partial(_outproj_kernel, tm=tm),
        grid=(T // tm,),
        in_specs=[row(CONV_DIM), row(MLA_DIM), row(D),
                  pl.BlockSpec((1, 6, D), lambda i: (i // tps, 0, 0)),
                  _resident((D, D)), _resident((1, D)),
                  _resident((N_EXPERTS, D)), _resident((N_EXPERTS, 1))],
        out_specs=[row(D), row(D), colblk(TOP_K), colblk(TOP_K)],
        out_shape=[jax.ShapeDtypeStruct((T, D), F32),
                   jax.ShapeDtypeStruct((T, D), F32),
                   jax.ShapeDtypeStruct((TOP_K, T), jnp.int32),
                   jax.ShapeDtypeStruct((TOP_K, T), F32)],
        compiler_params=_cparams(("arbitrary",)),
        name="outproj",
    )(mc, mm, x2, mod3, w_out_b, norm_g, rw_t, rb_col)


def _row_copy(src_hbm, src_row, dst, sem):
    return pltpu.make_async_copy(src_hbm.at[pl.ds(src_row, 1)], dst, sem)


ROW_ISSUE_UNROLL = 8


def _start_rows(idx_at, n_rows, src_hbm, dst, sem):
    def issue(r, c):
        _row_copy(src_hbm, idx_at(r), dst.at[pl.ds(r, 1)], sem).start()
        return c

    lax.fori_loop(0, n_rows, issue, 0, unroll=ROW_ISSUE_UNROLL)


def _wait_rows(n_rows, src_hbm, dst, sem):
    pltpu.make_async_copy(src_hbm.at[pl.ds(0, n_rows)], dst, sem).wait()


def _moe_kernel(te_ref, nu_ref, tok_ref, tok_next_ref, h_hbm, wgu_ref, bgu_ref, wd_ref, bd_ref,
                o_ref, xbuf, sem, *, tme, fc):
    i = pl.program_id(0)
    n_used = nu_ref[0]
    slot = i % 2

    @pl.when(i == 0)
    def _():
        _start_rows(lambda r: tok_ref[0, 0, r], tme, h_hbm, xbuf.at[0], sem.at[0])

    @pl.when(i + 1 < n_used)
    def _():
        _start_rows(lambda r: tok_next_ref[0, 0, r], tme, h_hbm, xbuf.at[1 - slot], sem.at[1 - slot])

    @pl.when(i < n_used)
    def _():
        _wait_rows(tme, h_hbm, xbuf.at[slot], sem.at[slot])
        x = xbuf[slot].astype(BF16)
        acts = []
        for c in range(D_EXPERT // fc):
            g = _dot(x, wgu_ref[0, :, c * fc:(c + 1) * fc]) + bgu_ref[0, :, c * fc:(c + 1) * fc]
            u = (_dot(x, wgu_ref[0, :, D_EXPERT + c * fc:D_EXPERT + (c + 1) * fc])
                 + bgu_ref[0, :, D_EXPERT + c * fc:D_EXPERT + (c + 1) * fc])
            g = jnp.minimum(g, SWIGLU_LIMIT)
            u = jnp.clip(u, -SWIGLU_LIMIT, SWIGLU_LIMIT)
            acts.append(((u + 1.0) * g * jax.nn.sigmoid(SWIGLU_ALPHA * g)).astype(BF16))
        o_ref[...] = _dot(jnp.concatenate(acts, axis=1), wd_ref[0]) + bd_ref[0]

    @pl.when(i >= n_used)
    def _():
        o_ref[...] = jnp.zeros(o_ref.shape, F32)


def _moe(tile_expert, n_used, buf_tok, h2, w_gu_b, b_gu, w_d_b, b_d, tme):
    n_tiles = tile_expert.shape[0]
    D = h2.shape[1]
    tok3 = buf_tok.reshape(n_tiles, 1, tme)
    one = pl.Buffered(1)
    grid_spec = pltpu.PrefetchScalarGridSpec(
        num_scalar_prefetch=2,
        grid=(n_tiles,),
        in_specs=[
            pl.BlockSpec((1, 1, tme), lambda i, te, nu: (i, 0, 0), memory_space=pltpu.SMEM),
            pl.BlockSpec((1, 1, tme), lambda i, te, nu: (jnp.minimum(i + 1, n_tiles - 1), 0, 0),
                         memory_space=pltpu.SMEM),
            pl.BlockSpec(memory_space=pl.ANY),
            pl.BlockSpec((1, D, 2 * D_EXPERT), lambda i, te, nu: (te[i], 0, 0), pipeline_mode=one),
            pl.BlockSpec((1, 1, 2 * D_EXPERT), lambda i, te, nu: (te[i], 0, 0)),
            pl.BlockSpec((1, D_EXPERT, D), lambda i, te, nu: (te[i], 0, 0), pipeline_mode=one),
            pl.BlockSpec((1, 1, D), lambda i, te, nu: (te[i], 0, 0)),
        ],
        out_specs=pl.BlockSpec((tme, D), lambda i, te, nu: (i, 0)),
        scratch_shapes=[pltpu.VMEM((2, tme, D), F32), pltpu.SemaphoreType.DMA((2,))],
    )
    return pl.pallas_call(
        functools.partial(_moe_kernel, tme=tme, fc=512),
        grid_spec=grid_spec,
        out_shape=jax.ShapeDtypeStruct((n_tiles * tme, D), F32),
        compiler_params=_cparams(("arbitrary",)),
        name="moe",
    )(tile_expert, n_used, tok3, tok3, h2,
      w_gu_b, b_gu.reshape(N_EXPERTS, 1, 2 * D_EXPERT), w_d_b, b_d.reshape(N_EXPERTS, 1, D))


def _combine_kernel(dest_ref, dest_next_ref, gate_ref, x1_ref, mod_ref, y_hbm, o_ref, ybuf, sem, *, tm, n_tiles):
    i = pl.program_id(0)
    slot = i % 2

    @pl.when(i == 0)
    def _():
        _start_rows(lambda r: dest_ref[0, 0, r], TOP_K * tm, y_hbm, ybuf.at[0], sem.at[0])

    @pl.when(i + 1 < n_tiles)
    def _():
        _start_rows(lambda r: dest_next_ref[0, 0, r], TOP_K * tm, y_hbm, ybuf.at[1 - slot], sem.at[1 - slot])

    _wait_rows(TOP_K * tm, y_hbm, ybuf.at[slot], sem.at[slot])
    gates = gate_ref[...]
    y = gates[:, 0:1] * ybuf[slot, 0:tm, :]
    for kk in range(1, TOP_K):
        y = y + gates[:, kk:kk + 1] * ybuf[slot, kk * tm:(kk + 1) * tm, :]
    g_f = mod_ref[0][5:6]
    o_ref[...] = x1_ref[...] + g_f * y


def _combine(dest_tiles, gates_t, x1, mod3, y_sorted, S, tm):
    T, D = x1.shape
    tps = S // tm
    n_tiles = T // tm
    return pl.pallas_call(
        functools.partial(_combine_kernel, tm=tm, n_tiles=n_tiles),
        grid=(n_tiles,),
        in_specs=[pl.BlockSpec((1, 1, TOP_K * tm), lambda i: (i, 0, 0), memory_space=pltpu.SMEM),
                  pl.BlockSpec((1, 1, TOP_K * tm), lambda i: (jnp.minimum(i + 1, n_tiles - 1), 0, 0),
                               memory_space=pltpu.SMEM),
                  pl.BlockSpec((tm, TOP_K), lambda i: (i, 0)),
                  pl.BlockSpec((tm, D), lambda i: (i, 0)),
                  pl.BlockSpec((1, 6, D), lambda i: (i // tps, 0, 0)),
                  pl.BlockSpec(memory_space=pl.ANY)],
        out_specs=pl.BlockSpec((tm, D), lambda i: (i, 0)),
        out_shape=jax.ShapeDtypeStruct((T, D), F32),
        scratch_shapes=[pltpu.VMEM((2, TOP_K * tm, D), F32), pltpu.SemaphoreType.DMA((2,))],
        compiler_params=_cparams(("arbitrary",)),
        name="combine",
    )(dest_tiles, dest_tiles, gates_t, x1, mod3, y_sorted)


def _route(top_i, tme):
    T = top_i.shape[1]
    TK = T * TOP_K
    flat_e = top_i.T.reshape(-1)
    onehot = (flat_e[:, None] == jnp.arange(N_EXPERTS, dtype=jnp.int32)[None, :]).astype(jnp.int32)
    csum = jnp.cumsum(onehot, axis=0)
    counts = csum[-1]
    padded = ((counts + tme - 1) // tme) * tme
    padded_end = jnp.cumsum(padded)
    padded_start = padded_end - padded
    group_start = jnp.cumsum(counts) - counts
    dest_orig = jnp.sum(onehot * (padded_start[None, :] + csum - onehot), axis=1).astype(jnp.int32)

    n_tiles = -(-TK // tme) + N_EXPERTS
    tile_start = jnp.arange(n_tiles) * tme
    tile_expert = jnp.minimum(jnp.searchsorted(padded_end, tile_start, side='right'),
                              N_EXPERTS - 1).astype(jnp.int32)
    n_used = (padded_end[-1] // tme).astype(jnp.int32).reshape(1)

    tok_sorted = (jnp.argsort(flat_e) // TOP_K).astype(jnp.int32)
    r = (tile_start - padded_start[tile_expert])[:, None] + jnp.arange(tme)[None, :]
    valid = r < counts[tile_expert][:, None]
    src = jnp.clip(group_start[tile_expert][:, None] + r, 0, TK - 1)
    buf_tok = jnp.where(valid, tok_sorted[src.reshape(-1)].reshape(n_tiles, tme), 0).reshape(-1)
    return tile_expert, n_used, buf_tok, dest_orig


def _permute_weights(w_in, w_uq, w_ukv, q_head_g, k_head_g):
    half = QK_ROPE // 2
    pe0 = _C_PE
    pe_cols = jnp.concatenate([jnp.arange(pe0, pe0 + half), jnp.arange(pe0, pe0 + half),
                               jnp.arange(pe0 + half, pe0 + 2 * half), jnp.arange(pe0 + half, pe0 + 2 * half)])
    w_in_p = jnp.concatenate([w_in[:, :pe0], w_in[:, pe_cols]], axis=1).astype(BF16)

    hq = jnp.arange(MLA_HEADS)[:, None] * QK_HEAD
    nope_cols = (hq + jnp.arange(QK_NOPE)[None, :]).reshape(-1)
    rope_cols = []
    for j in range(MLA_HEADS // 2):
        for part in range(2):
            for h in (2 * j, 2 * j + 1):
                rope_cols.append(h * QK_HEAD + QK_NOPE + part * half + jnp.arange(half))
    wuq_p = w_uq[:, jnp.concatenate([nope_cols] + rope_cols)].astype(BF16)

    hk = jnp.arange(MLA_HEADS)[:, None] * (QK_NOPE + V_HEAD)
    kn_cols = (hk + jnp.arange(QK_NOPE)[None, :]).reshape(-1)
    v_cols = (hk + QK_NOPE + jnp.arange(V_HEAD)[None, :]).reshape(-1)
    wukv_p = w_ukv[:, jnp.concatenate([kn_cols, v_cols])].astype(BF16)

    def gain_p(g):
        x1, x2 = g[QK_NOPE:QK_NOPE + half], g[QK_NOPE + half:]
        return jnp.concatenate([g[:QK_NOPE], x1, x1, x2, x2]).reshape(1, 2 * LANES)

    return w_in_p, wuq_p, wukv_p, gain_p(q_head_g), gain_p(k_head_g)


def _tile(n, pref):
    t = pref
    while n % t:
        t //= 2
    return t


def kernel(x, c, positions, w_ada, b_ada, norm_mix_g, w_in, conv_w, q_lat_norm_g, w_uq, kv_lat_norm_g, w_ukv, q_head_norm_g, k_head_norm_g, conv_out_norm_g, mla_out_norm_g, w_out, norm_ffn_g, router_w, router_b, w_gate_up, b_gate_up, w_down, b_down):
    B, S, D = x.shape
    T = B * S
    assert D == D_MODEL and w_ada.shape[0] == 1 and S % 128 == 0
    l = 0
    tm = _tile(S, 512)
    ts = 512 if S >= 8192 else S // 8
    tme = 512 if T * TOP_K >= 32768 else 256
    tmc = _tile(S, 256)

    mod3 = _ada(c, w_ada[l], b_ada[l]).reshape(B, 6, D)
    x2 = x.reshape(T, D)

    w_in_p, wuq_p, wukv_p, qg_p, kg_p = _permute_weights(
        w_in[l], w_uq[l], w_ukv[l], q_head_norm_g[l], k_head_norm_g[l])
    half = QK_ROPE // 2
    inv_freq = ROPE_THETA ** (-jnp.arange(half, dtype=F32) / half)
    invf = jnp.tile(inv_freq, 4).reshape(1, LANES)

    mc, qln, kvln, kpe = _inproj(
        x2, mod3, norm_mix_g[l].reshape(1, D), w_in_p, conv_w[l],
        conv_out_norm_g[l].reshape(1, CONV_DIM), q_lat_norm_g[l].reshape(1, Q_LORA),
        kv_lat_norm_g[l].reshape(1, KV_LORA), S, tm)
    q, k, v = _upproj(qln, kvln, kpe, positions.reshape(T, 1), invf, wuq_p, wukv_p, qg_p, kg_p, B, S, tm)
    mm = _attention(q, k, v, mla_out_norm_g[l], B, S, ts, 4, 2 * ts).reshape(T, MLA_DIM)
    x1, h2, top_i, gates = _outproj(
        mc, mm, x2, mod3, w_out[l].astype(BF16), norm_ffn_g[l].reshape(1, D),
        router_w[l].T, router_b[l].reshape(N_EXPERTS, 1), S, tm)

    tile_expert, n_used, buf_tok, dest_orig = _route(top_i, tme)
    y_sorted = _moe(tile_expert, n_used, buf_tok, h2, w_gate_up[l].astype(BF16), b_gate_up[l],
                    w_down[l].astype(BF16), b_down[l], tme)
    dest_tiles = dest_orig.reshape(T // tmc, tmc, TOP_K).transpose(0, 2, 1).reshape(T // tmc, 1, TOP_K * tmc)
    out = _combine(dest_tiles, gates.T, x1, mod3, y_sorted, S, tmc)
    return out.reshape(B, S, D)
```

```python
import functools

import jax
import jax.numpy as jnp
from jax import lax
from jax.experimental import pallas as pl
from jax.experimental.pallas import tpu as pltpu

F32 = jnp.float32
BF16 = jnp.bfloat16

D_MODEL = 2048
CONV_GROUPS = 8
CONV_DIM = 1024
CONV_K = 3
MLA_HEADS = 8
QK_NOPE = 128
QK_ROPE = 64
QK_HEAD = QK_NOPE + QK_ROPE
V_HEAD = 128
MLA_DIM = MLA_HEADS * V_HEAD
Q_LORA = 768
KV_LORA = 512
ROPE_THETA = 10000.0
N_EXPERTS = 32
TOP_K = 4
D_EXPERT = D_MODEL
SWIGLU_LIMIT = 7.0
SWIGLU_ALPHA = 1.702
EPS = 1e-6

LANES = 128
QK_PAD = 2 * LANES
V_EXT = 2 * LANES
LOG2E = 1.4426950408889634
_C_B, _C_C, _C_U = 0, CONV_DIM, 2 * CONV_DIM
_C_Q = 3 * CONV_DIM
_C_KV = _C_Q + Q_LORA
_C_PE = _C_KV + KV_LORA
D_IN_PAD = _C_PE + LANES
VMEM_LIMIT = 56 * 1024 * 1024


def _cparams(sem):
    return pltpu.CompilerParams(dimension_semantics=sem, vmem_limit_bytes=VMEM_LIMIT)


def _resident(shape):
    nd = len(shape)
    return pl.BlockSpec(shape, lambda *_: (0,) * nd, pipeline_mode=pl.Buffered(1))


def _split_bf16(a):
    hi = a.astype(BF16)
    lo = (a - hi.astype(F32)).astype(BF16)
    return hi, lo


def _dot(a, b):
    return jnp.dot(a, b, preferred_element_type=F32)


def _dot_nt(a, b):
    return lax.dot_general(a, b, (((1,), (1,)), ((), ())), preferred_element_type=F32)


def _rms(x, n):
    return lax.rsqrt(jnp.sum(x * x, axis=-1, keepdims=True) * (1.0 / n) + EPS)


def _ada_kernel(c_ref, w_ref, b_ref, o_ref):
    c = c_ref[...]
    cond = c * jax.nn.sigmoid(c)
    c_hi, c_lo = _split_bf16(cond)
    w_hi, w_lo = _split_bf16(w_ref[...])
    o_ref[...] = _dot(c_hi, w_hi) + _dot(c_hi, w_lo) + _dot(c_lo, w_hi) + b_ref[...]


def _ada(c, w_ada, b_ada):
    B, D = c.shape
    n_out = w_ada.shape[1]
    rows = 8
    c_pad = jnp.zeros((rows, D), F32).at[:B].set(c)
    tn = 1024
    out = pl.pallas_call(
        _ada_kernel,
        grid=(n_out // tn,),
        in_specs=[pl.BlockSpec((rows, D), lambda j: (0, 0)),
                  pl.BlockSpec((D, tn), lambda j: (0, j)),
                  pl.BlockSpec((1, tn), lambda j: (0, j))],
        out_specs=pl.BlockSpec((rows, tn), lambda j: (0, j)),
        out_shape=jax.ShapeDtypeStruct((rows, n_out), F32),
        compiler_params=_cparams(("arbitrary",)),
        name="ada",
    )(c_pad, w_ada, b_ada.reshape(1, n_out))
    return out[:B]


def _inproj_kernel(x_ref, mod_ref, g_ref, w_ref, cw_ref, cg_ref, qg_ref, kvg_ref,
                   mc_ref, qln_ref, kvln_ref, kpe_ref, cu_scr, *, tm, tiles_per_seq):
    i = pl.program_id(0)
    x = x_ref[...]
    mod = mod_ref[0]
    sh, sc = mod[0:1], mod[1:2]
    h = (x * _rms(x, D_MODEL) * g_ref[...]) * (1.0 + sc) + sh
    hb = h.astype(BF16)

    def proj(lo, hi):
        return _dot(hb, w_ref[:, lo:hi])

    cu = proj(_C_C, _C_U) * proj(_C_U, _C_Q)

    @pl.when(i % tiles_per_seq == 0)
    def _():
        cu_scr[0:8, :] = jnp.zeros((8, CONV_DIM), F32)

    cu_scr[8:8 + tm, :] = cu
    s1 = cu_scr[7:7 + tm, :]
    s2 = cu_scr[6:6 + tm, :]
    cw = cw_ref[...]
    conv = cw[0:1] * s2 + cw[1:2] * s1 + cw[2:3] * cu
    cu_scr[0:8, :] = cu_scr[tm:tm + 8, :]
    y = proj(_C_B, _C_C) * conv
    cg = cg_ref[...]
    for g in range(CONV_GROUPS):
        sl = slice(g * LANES, (g + 1) * LANES)
        yg = y[:, sl]
        mc_ref[:, sl] = (yg * _rms(yg, LANES) * cg[:, sl]).astype(BF16)

    ql = proj(_C_Q, _C_KV)
    qln_ref[...] = (ql * _rms(ql, Q_LORA) * qg_ref[...]).astype(BF16)
    kvl = proj(_C_KV, _C_PE)
    kvln_ref[...] = (kvl * _rms(kvl, KV_LORA) * kvg_ref[...]).astype(BF16)
    kpe_ref[...] = proj(_C_PE, D_IN_PAD)


def _inproj(x2, mod3, norm_g, w_in_p, conv_w, conv_g, q_lat_g, kv_lat_g, S, tm):
    T, D = x2.shape
    tps = S // tm
    row = lambda n: pl.BlockSpec((tm, n), lambda i: (i, 0))
    return pl.pallas_call(
        functools.partial(_inproj_kernel, tm=tm, tiles_per_seq=tps),
        grid=(T // tm,),
        in_specs=[row(D),
                  pl.BlockSpec((1, 6, D), lambda i: (i // tps, 0, 0)),
                  _resident((1, D)),
                  _resident((D, D_IN_PAD)),
                  _resident((CONV_K, CONV_DIM)),
                  _resident((1, CONV_DIM)),
                  _resident((1, Q_LORA)),
                  _resident((1, KV_LORA))],
        out_specs=[row(CONV_DIM), row(Q_LORA), row(KV_LORA), row(LANES)],
        out_shape=[jax.ShapeDtypeStruct((T, CONV_DIM), BF16),
                   jax.ShapeDtypeStruct((T, Q_LORA), BF16),
                   jax.ShapeDtypeStruct((T, KV_LORA), BF16),
                   jax.ShapeDtypeStruct((T, LANES), F32)],
        scratch_shapes=[pltpu.VMEM((tm + 8, CONV_DIM), F32)],
        compiler_params=_cparams(("arbitrary",)),
        name="inproj",
    )(x2, mod3, norm_g, w_in_p, conv_w, conv_g, q_lat_g, kv_lat_g)


def _upproj_kernel(qln_ref, kvln_ref, kpe_ref, pos_ref, invf_ref, wuq_ref, wukv_ref,
                   qg_ref, kg_ref, q_ref, k_ref, v_ref, *, tm):
    ang = pos_ref[...].astype(F32) * invf_ref[...]
    lane = lax.broadcasted_iota(jnp.int32, (tm, LANES), 1)
    cos = jnp.cos(ang)
    sin_signed = jnp.where(lane < LANES // 2, -1.0, 1.0) * jnp.sin(ang)
    even_head = (lane // (QK_ROPE // 2)) % 2 == 0

    def rope(col):
        return col * cos + pltpu.roll(col, LANES // 2, axis=1) * sin_signed

    qg = qg_ref[...]
    kg = kg_ref[...]
    qscale = QK_HEAD ** -0.5 * LOG2E

    q = _dot(qln_ref[...], wuq_ref[...])
    for j in range(MLA_HEADS // 2):
        rc = q[:, MLA_DIM + j * LANES:MLA_DIM + (j + 1) * LANES]
        rc2 = rc * rc
        ss_even = jnp.sum(jnp.where(even_head, rc2, 0.0), axis=-1, keepdims=True)
        ss_odd = jnp.sum(jnp.where(even_head, 0.0, rc2), axis=-1, keepdims=True)
        r = []
        for h, ss_rope in ((2 * j, ss_even), (2 * j + 1, ss_odd)):
            qn = q[:, h * LANES:(h + 1) * LANES]
            ss = jnp.sum(qn * qn, axis=-1, keepdims=True) + ss_rope
            rh = lax.rsqrt(ss * (1.0 / QK_HEAD) + EPS) * qscale
            r.append(rh)
            q_ref[0, h, :, 0:LANES] = (qn * rh * qg[:, 0:LANES]).astype(BF16)
        roped = rope(rc * jnp.where(even_head, r[0], r[1]) * qg[:, LANES:2 * LANES]).astype(BF16)
        q_ref[0, 2 * j, :, LANES:2 * LANES] = roped
        q_ref[0, 2 * j + 1, :, LANES:2 * LANES] = roped

    kv = _dot(kvln_ref[...], wukv_ref[...])
    kp = kpe_ref[...]
    ss_pe = 0.5 * jnp.sum(kp * kp, axis=-1, keepdims=True)
    kr = rope(kp * kg[:, LANES:2 * LANES])
    for h in range(MLA_HEADS):
        kn = kv[:, h * LANES:(h + 1) * LANES]
        ss = jnp.sum(kn * kn, axis=-1, keepdims=True) + ss_pe
        rh = lax.rsqrt(ss * (1.0 / QK_HEAD) + EPS)
        k_ref[0, h, :, 0:LANES] = (kn * rh * kg[:, 0:LANES]).astype(BF16)
        mine = even_head if h % 2 == 0 else jnp.logical_not(even_head)
        k_ref[0, h, :, LANES:2 * LANES] = jnp.where(mine, kr * rh, 0.0).astype(BF16)
        v_ref[0, h, :, 0:V_HEAD] = kv[:, MLA_DIM + h * LANES:MLA_DIM + (h + 1) * LANES].astype(BF16)
        v_ref[0, h, :, V_HEAD:V_EXT] = jnp.ones((tm, V_EXT - V_HEAD), BF16)


def _upproj(qln, kvln, kpe, pos_col, invf, wuq_p, wukv_p, qg_p, kg_p, B, S, tm):
    T = qln.shape[0]
    tps = S // tm
    row = lambda n: pl.BlockSpec((tm, n), lambda i: (i, 0))
    head = lambda n: pl.BlockSpec((1, MLA_HEADS, tm, n), lambda i: (i // tps, 0, i % tps, 0))
    return pl.pallas_call(
        functools.partial(_upproj_kernel, tm=tm),
        grid=(T // tm,),
        in_specs=[row(Q_LORA), row(KV_LORA), row(LANES), row(1),
                  _resident((1, LANES)),
                  _resident(wuq_p.shape), _resident(wukv_p.shape),
                  _resident((1, 2 * LANES)), _resident((1, 2 * LANES))],
        out_specs=[head(QK_PAD), head(QK_PAD), head(V_EXT)],
        out_shape=[jax.ShapeDtypeStruct((B, MLA_HEADS, S, QK_PAD), BF16),
                   jax.ShapeDtypeStruct((B, MLA_HEADS, S, QK_PAD), BF16),
                   jax.ShapeDtypeStruct((B, MLA_HEADS, S, V_EXT), BF16)],
        compiler_params=_cparams(("arbitrary",)),
        name="upproj",
    )(qln, kvln, kpe, pos_col, invf, wuq_p, wukv_p, qg_p, kg_p)


def _attn_kernel(q_ref, k_ref, v_ref, g_ref, o_ref, m_scr, acc_scr, *, ts, nsub, tk):
    qi = pl.program_id(1)
    tq = ts * nsub
    for a in range(nsub):
        m_scr[a] = jnp.full((ts, LANES), -jnp.inf, F32)
        acc_scr[a] = jnp.zeros((ts, V_EXT), F32)

    def step(a, start, width, masked):
        q = q_ref[0, a * ts:(a + 1) * ts, :]
        k = k_ref[0, pl.ds(start, width), :]
        v = v_ref[0, pl.ds(start, width), :]
        s = _dot_nt(q, k)
        if masked:
            row = lax.broadcasted_iota(jnp.int32, (ts, width), 0) + (qi * tq + a * ts)
            col = lax.broadcasted_iota(jnp.int32, (ts, width), 1) + start
            s = jnp.where(col <= row, s, -jnp.inf)
        m_prev = m_scr[a]
        m_new = jnp.maximum(m_prev, jnp.max(s, axis=-1, keepdims=True))
        p = jnp.exp2(s - jnp.concatenate([m_new] * (width // LANES), axis=1))
        corr = jnp.exp2(m_prev - m_new)
        acc_scr[a] = (jnp.concatenate([corr] * (V_EXT // LANES), axis=1) * acc_scr[a]
                      + _dot(p.astype(BF16), v))
        m_scr[a] = m_new

    def body(j, carry):
        start = pl.multiple_of(j * tk, tk)
        for a in range(nsub):
            step(a, start, tk, False)
        return carry

    lax.fori_loop(0, qi * (tq // tk), body, 0)
    for a in range(nsub):
        for d in range(a + 1):
            step(a, pl.multiple_of(qi * tq + d * ts, ts), ts, d == a)

    for a in range(nsub):
        acc = acc_scr[a]
        o = acc[:, 0:V_HEAD] / acc[:, V_HEAD:V_EXT]
        o_ref[0, a * ts:(a + 1) * ts, :] = (o * _rms(o, V_HEAD) * g_ref[0]).astype(BF16)


def _attention(q, k, v, mla_g, B, S, ts, nsub, tk):
    BH = B * MLA_HEADS
    H = MLA_HEADS
    tq = ts * nsub
    return pl.pallas_call(
        functools.partial(_attn_kernel, ts=ts, nsub=nsub, tk=tk),
        grid=(BH, S // tq),
        in_specs=[pl.BlockSpec((1, tq, QK_PAD), lambda bh, qi: (bh, qi, 0)),
                  pl.BlockSpec((1, S, QK_PAD), lambda bh, qi: (bh, 0, 0)),
                  pl.BlockSpec((1, S, V_EXT), lambda bh, qi: (bh, 0, 0)),
                  pl.BlockSpec((1, 1, V_HEAD), lambda bh, qi: (bh % H, 0, 0))],
        out_specs=pl.BlockSpec((1, tq, V_HEAD), lambda bh, qi: (bh // H, qi, bh % H)),
        out_shape=jax.ShapeDtypeStruct((B, S, MLA_DIM), BF16),
        scratch_shapes=[pltpu.VMEM((nsub, ts, LANES), F32), pltpu.VMEM((nsub, ts, V_EXT), F32)],
        compiler_params=_cparams(("arbitrary", "arbitrary")),
        name="attn",
    )(q.reshape(BH, S, QK_PAD), k.reshape(BH, S, QK_PAD), v.reshape(BH, S, V_EXT),
      mla_g.reshape(H, 1, V_HEAD))


def _outproj_kernel(mc_ref, mm_ref, x_ref, mod_ref, w_ref, g_ref, rw_ref, rb_ref,
                    x1_ref, h2_ref, idx_ref, gate_ref, *, tm):
    mod = mod_ref[0]
    g_m, sh_f, sc_f = mod[2:3], mod[3:4], mod[4:5]
    mix = _dot(mc_ref[...], w_ref[0:CONV_DIM, :]) + _dot(mm_ref[...], w_ref[CONV_DIM:, :])
    x1 = x_ref[...] + g_m * mix
    x1_ref[...] = x1
    h2 = (x1 * _rms(x1, D_MODEL) * g_ref[...]) * (1.0 + sc_f) + sh_f
    h2_ref[...] = h2

    h_hi, h_lo = _split_bf16(h2)
    r_hi, r_lo = _split_bf16(rw_ref[...])
    logits = _dot_nt(r_hi, h_hi) + _dot_nt(r_lo, h_hi) + _dot_nt(r_hi, h_lo) + rb_ref[...]
    eid = lax.broadcasted_iota(jnp.int32, (N_EXPERTS, tm), 0)
    vals = []
    for kk in range(TOP_K):
        mx = jnp.max(logits, axis=0, keepdims=True)
        idx = jnp.min(jnp.where(logits == mx, eid, N_EXPERTS), axis=0, keepdims=True)
        idx_ref[kk:kk + 1, :] = idx
        vals.append(mx)
        logits = jnp.where(eid == idx, -jnp.inf, logits)
    ex = [jnp.exp(vv - vals[0]) for vv in vals]
    denom = ex[0] + ex[1] + ex[2] + ex[3]
    for kk in range(TOP_K):
        gate_ref[kk:kk + 1, :] = ex[kk] / denom


def _outproj(mc, mm, x2, mod3, w_out_b, norm_g, rw_t, rb_col, S, tm):
    T, D = x2.shape
    tps = S // tm
    row = lambda n: pl.BlockSpec((tm, n), lambda i: (i, 0))
    colblk = lambda n: pl.BlockSpec((n, tm), lambda i: (0, i))
    return pl.pallas_call(
        functools.partial(_outproj_kernel, tm=tm),
        grid=(T // tm,),
        in_specs=[row(CONV_DIM), row(MLA_DIM), row(D),
                  pl.BlockSpec((1, 6, D), lambda i: (i // tps, 0, 0)),
                  _resident((D, D)), _resident((1, D)),
                  _resident((N_EXPERTS, D)), _resident((N_EXPERTS, 1))],
        out_specs=[row(D), row(D), colblk(TOP_K), colblk(TOP_K)],
        out_shape=[jax.ShapeDtypeStruct((T, D), F32),
                   jax.ShapeDtypeStruct((T, D), F32),
                   jax.ShapeDtypeStruct((TOP_K, T), jnp.int32),
                   jax.ShapeDtypeStruct((TOP_K, T), F32)],
        compiler_params=_cparams(("arbitrary",)),
        name="outproj",
    )(mc, mm, x2, mod3, w_out_b, norm_g, rw_t, rb_col)


def _row_copy(src_hbm, src_row, dst, sem):
    return pltpu.make_async_copy(src_hbm.at[pl.ds(src_row, 1)], dst, sem)


ROW_ISSUE_UNROLL = 8


def _start_rows(idx_at, n_rows, src_hbm, dst, sem):
    def issue(r, c):
        _row_copy(src_hbm, idx_at(r), dst.at[pl.ds(r, 1)], sem).start()
        return c

    lax.fori_loop(0, n_rows, issue, 0, unroll=ROW_ISSUE_UNROLL)


def _wait_rows(n_rows, src_hbm, dst, sem):
    pltpu.make_async_copy(src_hbm.at[pl.ds(0, n_rows)], dst, sem).wait()


def _moe_kernel(te_ref, nu_ref, tok_ref, tok_next_ref, h_hbm, wgu_ref, bgu_ref, wd_ref, bd_ref,
                o_ref, xbuf, sem, *, tme, fc):
    i = pl.program_id(0)
    n_used = nu_ref[0]
    slot = i % 2

    @pl.when(i == 0)
    def _():
        _start_rows(lambda r: tok_ref[0, 0, r], tme, h_hbm, xbuf.at[0], sem.at[0])

    @pl.when(i < n_used)
    def _():
        _wait_rows(tme, h_hbm, xbuf.at[slot], sem.at[slot])
        x = xbuf[slot].astype(BF16)
        acts = []
        n_chunks = D_EXPERT // fc
        per = tme // n_chunks
        for c in range(n_chunks):
            for r in range(c * per, (c + 1) * per):
                _row_copy(h_hbm, tok_next_ref[0, 0, r], xbuf.at[1 - slot, pl.ds(r, 1)], sem.at[1 - slot]).start()
            g = _dot(x, wgu_ref[0, :, c * fc:(c + 1) * fc]) + bgu_ref[0, :, c * fc:(c + 1) * fc]
            u = (_dot(x, wgu_ref[0, :, D_EXPERT + c * fc:D_EXPERT + (c + 1) * fc])
                 + bgu_ref[0, :, D_EXPERT + c * fc:D_EXPERT + (c + 1) * fc])
            g = jnp.minimum(g, SWIGLU_LIMIT)
            u = jnp.clip(u, -SWIGLU_LIMIT, SWIGLU_LIMIT)
            acts.append(((u + 1.0) * g * jax.nn.sigmoid(SWIGLU_ALPHA * g)).astype(BF16))
        o_ref[...] = _dot(jnp.concatenate(acts, axis=1), wd_ref[0]) + bd_ref[0]

    @pl.when(i == n_used)
    def _():
        _wait_rows(tme, h_hbm, xbuf.at[slot], sem.at[slot])

    @pl.when(i >= n_used)
    def _():
        o_ref[...] = jnp.zeros(o_ref.shape, F32)


def _moe(tile_expert, n_used, buf_tok, h2, w_gu_b, b_gu, w_d_b, b_d, tme):
    n_tiles = tile_expert.shape[0]
    D = h2.shape[1]
    tok3 = buf_tok.reshape(n_tiles, 1, tme)
    one = pl.Buffered(1)
    grid_spec = pltpu.PrefetchScalarGridSpec(
        num_scalar_prefetch=2,
        grid=(n_tiles,),
        in_specs=[
            pl.BlockSpec((1, 1, tme), lambda i, te, nu: (i, 0, 0), memory_space=pltpu.SMEM),
            pl.BlockSpec((1, 1, tme), lambda i, te, nu: (jnp.minimum(i + 1, n_tiles - 1), 0, 0),
                         memory_space=pltpu.SMEM),
            pl.BlockSpec(memory_space=pl.ANY),
            pl.BlockSpec((1, D, 2 * D_EXPERT), lambda i, te, nu: (te[i], 0, 0), pipeline_mode=one),
            pl.BlockSpec((1, 1, 2 * D_EXPERT), lambda i, te, nu: (te[i], 0, 0)),
            pl.BlockSpec((1, D_EXPERT, D), lambda i, te, nu: (te[i], 0, 0), pipeline_mode=one),
            pl.BlockSpec((1, 1, D), lambda i, te, nu: (te[i], 0, 0)),
        ],
        out_specs=pl.BlockSpec((tme, D), lambda i, te, nu: (i, 0)),
        scratch_shapes=[pltpu.VMEM((2, tme, D), F32), pltpu.SemaphoreType.DMA((2,))],
    )
    return pl.pallas_call(
        functools.partial(_moe_kernel, tme=tme, fc=512),
        grid_spec=grid_spec,
        out_shape=jax.ShapeDtypeStruct((n_tiles * tme, D), F32),
        compiler_params=_cparams(("arbitrary",)),
        name="moe",
    )(tile_expert, n_used, tok3, tok3, h2,
      w_gu_b, b_gu.reshape(N_EXPERTS, 1, 2 * D_EXPERT), w_d_b, b_d.reshape(N_EXPERTS, 1, D))


def _combine_kernel(dest_ref, dest_next_ref, gate_ref, x1_ref, mod_ref, y_hbm, o_ref, ybuf, sem, *, tm, n_tiles):
    i = pl.program_id(0)
    slot = i % 2

    @pl.when(i == 0)
    def _():
        _start_rows(lambda r: dest_ref[0, 0, r], TOP_K * tm, y_hbm, ybuf.at[0], sem.at[0])

    @pl.when(i + 1 < n_tiles)
    def _():
        _start_rows(lambda r: dest_next_ref[0, 0, r], TOP_K * tm, y_hbm, ybuf.at[1 - slot], sem.at[1 - slot])

    _wait_rows(TOP_K * tm, y_hbm, ybuf.at[slot], sem.at[slot])
    gates = gate_ref[...]
    y = gates[:, 0:1] * ybuf[slot, 0:tm, :]
    for kk in range(1, TOP_K):
        y = y + gates[:, kk:kk + 1] * ybuf[slot, kk * tm:(kk + 1) * tm, :]
    g_f = mod_ref[0][5:6]
    o_ref[...] = x1_ref[...] + g_f * y


def _combine(dest_tiles, gates_t, x1, mod3, y_sorted, S, tm):
    T, D = x1.shape
    tps = S // tm
    n_tiles = T // tm
    return pl.pallas_call(
        functools.partial(_combine_kernel, tm=tm, n_tiles=n_tiles),
        grid=(n_tiles,),
        in_specs=[pl.BlockSpec((1, 1, TOP_K * tm), lambda i: (i, 0, 0), memory_space=pltpu.SMEM),
                  pl.BlockSpec((1, 1, TOP_K * tm), lambda i: (jnp.minimum(i + 1, n_tiles - 1), 0, 0),
                               memory_space=pltpu.SMEM),
                  pl.BlockSpec((tm, TOP_K), lambda i: (i, 0)),
                  pl.BlockSpec((tm, D), lambda i: (i, 0)),
                  pl.BlockSpec((1, 6, D), lambda i: (i // tps, 0, 0)),
                  pl.BlockSpec(memory_space=pl.ANY)],
        out_specs=pl.BlockSpec((tm, D), lambda i: (i, 0)),
        out_shape=jax.ShapeDtypeStruct((T, D), F32),
        scratch_shapes=[pltpu.VMEM((2, TOP_K * tm, D), F32), pltpu.SemaphoreType.DMA((2,))],
        compiler_params=_cparams(("arbitrary",)),
        name="combine",
    )(dest_tiles, dest_tiles, gates_t, x1, mod3, y_sorted)


def _route(top_i, tme):
    T = top_i.shape[1]
    TK = T * TOP_K
    assert TK % tme == 0
    flat_e = top_i.T.reshape(-1)
    is_e = flat_e[:, None] == jnp.arange(N_EXPERTS, dtype=jnp.int32)[None, :]
    onehot = is_e.astype(jnp.int32)
    blk = 512 if TK % 512 == 0 else TK
    oh_b = is_e.astype(BF16).reshape(TK // blk, blk, N_EXPERTS)
    tri = (jnp.arange(blk)[:, None] >= jnp.arange(blk)[None, :]).astype(BF16)
    within = jnp.einsum('ij,bjk->bik', tri, oh_b, preferred_element_type=F32).astype(jnp.int32)
    blk_tot = within[:, -1, :]
    csum = (within + (jnp.cumsum(blk_tot, axis=0) - blk_tot)[:, None, :]).reshape(TK, N_EXPERTS)
    counts = csum[-1]
    padded = ((counts + tme - 1) // tme) * tme
    padded_end = jnp.cumsum(padded)
    padded_start = padded_end - padded
    group_start = jnp.cumsum(counts) - counts
    dest_orig = jnp.sum(onehot * (padded_start[None, :] + csum - onehot), axis=1).astype(jnp.int32)

    n_tiles = -(-TK // tme) + N_EXPERTS
    tile_start = jnp.arange(n_tiles) * tme
    tile_expert = jnp.minimum(jnp.sum(tile_start[:, None] >= padded_end[None, :], axis=1),
                              N_EXPERTS - 1).astype(jnp.int32)
    n_used = (padded_end[-1] // tme).astype(jnp.int32).reshape(1)

    tok_sorted = (jnp.argsort(flat_e) // TOP_K).astype(jnp.int32)
    r = (tile_start - padded_start[tile_expert])[:, None] + jnp.arange(tme)[None, :]
    valid = r < counts[tile_expert][:, None]
    src = jnp.clip(group_start[tile_expert][:, None] + r, 0, TK - 1)
    buf_tok = jnp.where(valid, tok_sorted[src.reshape(-1)].reshape(n_tiles, tme), 0).reshape(-1)
    return tile_expert, n_used, buf_tok, dest_orig


def _permute_weights(w_in, w_uq, w_ukv, q_head_g, k_head_g):
    half = QK_ROPE // 2
    pe0 = _C_PE
    pe_cols = jnp.concatenate([jnp.arange(pe0, pe0 + half), jnp.arange(pe0, pe0 + half),
                               jnp.arange(pe0 + half, pe0 + 2 * half), jnp.arange(pe0 + half, pe0 + 2 * half)])
    w_in_p = jnp.concatenate([w_in[:, :pe0], w_in[:, pe_cols]], axis=1).astype(BF16)

    hq = jnp.arange(MLA_HEADS)[:, None] * QK_HEAD
    nope_cols = (hq + jnp.arange(QK_NOPE)[None, :]).reshape(-1)
    rope_cols = []
    for j in range(MLA_HEADS // 2):
        for part in range(2):
            for h in (2 * j, 2 * j + 1):
                rope_cols.append(h * QK_HEAD + QK_NOPE + part * half + jnp.arange(half))
    wuq_p = w_uq[:, jnp.concatenate([nope_cols] + rope_cols)].astype(BF16)

    hk = jnp.arange(MLA_HEADS)[:, None] * (QK_NOPE + V_HEAD)
    kn_cols = (hk + jnp.arange(QK_NOPE)[None, :]).reshape(-1)
    v_cols = (hk + QK_NOPE + jnp.arange(V_HEAD)[None, :]).reshape(-1)
    wukv_p = w_ukv[:, jnp.concatenate([kn_cols, v_cols])].astype(BF16)

    def gain_p(g):
        x1, x2 = g[QK_NOPE:QK_NOPE + half], g[QK_NOPE + half:]
        return jnp.concatenate([g[:QK_NOPE], x1, x1, x2, x2]).reshape(1, 2 * LANES)

    return w_in_p, wuq_p, wukv_p, gain_p(q_head_g), gain_p(k_head_g)


def _tile(n, pref):
    t = pref
    while n % t:
        t //= 2
    return t


def kernel(x, c, positions, w_ada, b_ada, norm_mix_g, w_in, conv_w, q_lat_norm_g, w_uq, kv_lat_norm_g, w_ukv, q_head_norm_g, k_head_norm_g, conv_out_norm_g, mla_out_norm_g, w_out, norm_ffn_g, router_w, router_b, w_gate_up, b_gate_up, w_down, b_down):
    B, S, D = x.shape
    T = B * S
    assert D == D_MODEL and w_ada.shape[0] == 1 and S % 128 == 0
    l = 0
    tm = _tile(S, 512)
    ts = 512 if S >= 8192 else S // 8
    tme = 512 if T * TOP_K >= 32768 else 256
    tmc = _tile(S, 256)

    mod3 = _ada(c, w_ada[l], b_ada[l]).reshape(B, 6, D)
    x2 = x.reshape(T, D)

    w_in_p, wuq_p, wukv_p, qg_p, kg_p = _permute_weights(
        w_in[l], w_uq[l], w_ukv[l], q_head_norm_g[l], k_head_norm_g[l])
    half = QK_ROPE // 2
    inv_freq = ROPE_THETA ** (-jnp.arange(half, dtype=F32) / half)
    invf = jnp.tile(inv_freq, 4).reshape(1, LANES)

    mc, qln, kvln, kpe = _inproj(
        x2, mod3, norm_mix_g[l].reshape(1, D), w_in_p, conv_w[l],
        conv_out_norm_g[l].reshape(1, CONV_DIM), q_lat_norm_g[l].reshape(1, Q_LORA),
        kv_lat_norm_g[l].reshape(1, KV_LORA), S, tm)
    q, k, v = _upproj(qln, kvln, kpe, positions.reshape(T, 1), invf, wuq_p, wukv_p, qg_p, kg_p, B, S, tm)
    mm = _attention(q, k, v, mla_out_norm_g[l], B, S, ts, 4, 4 * ts).reshape(T, MLA_DIM)
    x1, h2, top_i, gates = _outproj(
        mc, mm, x2, mod3, w_out[l].astype(BF16), norm_ffn_g[l].reshape(1, D),
        router_w[l].T, router_b[l].reshape(N_EXPERTS, 1), S, tm)

    tile_expert, n_used, buf_tok, dest_orig = _route(top_i, tme)
    y_sorted = _moe(tile_expert, n_used, buf_tok, h2, w_gate_up[l].astype(BF16), b_gate_up[l],
                    w_down[l].astype(BF16), b_down[l], tme)
    dest_tiles = dest_orig.reshape(T // tmc, tmc, TOP_K).transpose(0, 2, 1).reshape(T // tmc, 1, TOP_K * tmc)
    out = _combine(dest_tiles, gates.T, x1, mod3, y_sorted, S, tmc)
    return out.reshape(B, S, D)
```

```python
import functools

import jax
import jax.numpy as jnp
from jax import lax
from jax.experimental import pallas as pl
from jax.experimental.pallas import tpu as pltpu

F32 = jnp.float32
BF16 = jnp.bfloat16

D_MODEL = 2048
CONV_GROUPS = 8
CONV_DIM = 1024
CONV_K = 3
MLA_HEADS = 8
QK_NOPE = 128
QK_ROPE = 64
QK_HEAD = QK_NOPE + QK_ROPE
V_HEAD = 128
MLA_DIM = MLA_HEADS * V_HEAD
Q_LORA = 768
KV_LORA = 512
ROPE_THETA = 10000.0
N_EXPERTS = 32
TOP_K = 4
D_EXPERT = D_MODEL
SWIGLU_LIMIT = 7.0
SWIGLU_ALPHA = 1.702
EPS = 1e-6

LANES = 128
QK_PAD = 2 * LANES
V_EXT = 2 * LANES
LOG2E = 1.4426950408889634
_C_B, _C_C, _C_U = 0, CONV_DIM, 2 * CONV_DIM
_C_Q = 3 * CONV_DIM
_C_KV = _C_Q + Q_LORA
_C_PE = _C_KV + KV_LORA
D_IN_PAD = _C_PE + LANES
VMEM_LIMIT = 56 * 1024 * 1024


def _cparams(sem):
    return pltpu.CompilerParams(dimension_semantics=sem, vmem_limit_bytes=VMEM_LIMIT)


def _resident(shape):
    nd = len(shape)
    return pl.BlockSpec(shape, lambda *_: (0,) * nd, pipeline_mode=pl.Buffered(1))


def _split_bf16(a):
    hi = a.astype(BF16)
    lo = (a - hi.astype(F32)).astype(BF16)
    return hi, lo


def _dot(a, b):
    return jnp.dot(a, b, preferred_element_type=F32)


def _dot_nt(a, b):
    return lax.dot_general(a, b, (((1,), (1,)), ((), ())), preferred_element_type=F32)


def _rms(x, n):
    return lax.rsqrt(jnp.sum(x * x, axis=-1, keepdims=True) * (1.0 / n) + EPS)


def _ada_kernel(c_ref, w_ref, b_ref, o_ref):
    c = c_ref[...]
    cond = c * jax.nn.sigmoid(c)
    c_hi, c_lo = _split_bf16(cond)
    w_hi, w_lo = _split_bf16(w_ref[...])
    o_ref[...] = _dot(c_hi, w_hi) + _dot(c_hi, w_lo) + _dot(c_lo, w_hi) + b_ref[...]


def _ada(c, w_ada, b_ada):
    B, D = c.shape
    n_out = w_ada.shape[1]
    rows = 8
    c_pad = jnp.zeros((rows, D), F32).at[:B].set(c)
    tn = 1024
    out = pl.pallas_call(
        _ada_kernel,
        grid=(n_out // tn,),
        in_specs=[pl.BlockSpec((rows, D), lambda j: (0, 0)),
                  pl.BlockSpec((D, tn), lambda j: (0, j)),
                  pl.BlockSpec((1, tn), lambda j: (0, j))],
        out_specs=pl.BlockSpec((rows, tn), lambda j: (0, j)),
        out_shape=jax.ShapeDtypeStruct((rows, n_out), F32),
        compiler_params=_cparams(("arbitrary",)),
        name="ada",
    )(c_pad, w_ada, b_ada.reshape(1, n_out))
    return out[:B]


def _inproj_kernel(x_ref, mod_ref, g_ref, w_ref, cw_ref, cg_ref, qg_ref, kvg_ref,
                   mc_ref, qln_ref, kvln_ref, kpe_ref, cu_scr, *, tm, tiles_per_seq):
    i = pl.program_id(0)
    x = x_ref[...]
    mod = mod_ref[0]
    sh, sc = mod[0:1], mod[1:2]
    h = (x * _rms(x, D_MODEL) * g_ref[...]) * (1.0 + sc) + sh
    hb = h.astype(BF16)

    def proj(lo, hi):
        return _dot(hb, w_ref[:, lo:hi])

    cu = proj(_C_C, _C_U) * proj(_C_U, _C_Q)

    @pl.when(i % tiles_per_seq == 0)
    def _():
        cu_scr[0:8, :] = jnp.zeros((8, CONV_DIM), F32)

    cu_scr[8:8 + tm, :] = cu
    s1 = cu_scr[7:7 + tm, :]
    s2 = cu_scr[6:6 + tm, :]
    cw = cw_ref[...]
    conv = cw[0:1] * s2 + cw[1:2] * s1 + cw[2:3] * cu
    cu_scr[0:8, :] = cu_scr[tm:tm + 8, :]
    y = proj(_C_B, _C_C) * conv
    cg = cg_ref[...]
    for g in range(CONV_GROUPS):
        sl = slice(g * LANES, (g + 1) * LANES)
        yg = y[:, sl]
        mc_ref[:, sl] = (yg * _rms(yg, LANES) * cg[:, sl]).astype(BF16)

    ql = proj(_C_Q, _C_KV)
    qln_ref[...] = (ql * _rms(ql, Q_LORA) * qg_ref[...]).astype(BF16)
    kvl = proj(_C_KV, _C_PE)
    kvln_ref[...] = (kvl * _rms(kvl, KV_LORA) * kvg_ref[...]).astype(BF16)
    kpe_ref[...] = proj(_C_PE, D_IN_PAD)


def _inproj(x2, mod3, norm_g, w_in_p, conv_w, conv_g, q_lat_g, kv_lat_g, S, tm):
    T, D = x2.shape
    tps = S // tm
    row = lambda n: pl.BlockSpec((tm, n), lambda i: (i, 0))
    return pl.pallas_call(
        functools.partial(_inproj_kernel, tm=tm, tiles_per_seq=tps),
        grid=(T // tm,),
        in_specs=[row(D),
                  pl.BlockSpec((1, 6, D), lambda i: (i // tps, 0, 0)),
                  _resident((1, D)),
                  _resident((D, D_IN_PAD)),
                  _resident((CONV_K, CONV_DIM)),
                  _resident((1, CONV_DIM)),
                  _resident((1, Q_LORA)),
                  _resident((1, KV_LORA))],
        out_specs=[row(CONV_DIM), row(Q_LORA), row(KV_LORA), row(LANES)],
        out_shape=[jax.ShapeDtypeStruct((T, CONV_DIM), BF16),
                   jax.ShapeDtypeStruct((T, Q_LORA), BF16),
                   jax.ShapeDtypeStruct((T, KV_LORA), BF16),
                   jax.ShapeDtypeStruct((T, LANES), F32)],
        scratch_shapes=[pltpu.VMEM((tm + 8, CONV_DIM), F32)],
        compiler_params=_cparams(("arbitrary",)),
        name="inproj",
    )(x2, mod3, norm_g, w_in_p, conv_w, conv_g, q_lat_g, kv_lat_g)


def _upproj_kernel(qln_ref, kvln_ref, kpe_ref, pos_ref, invf_ref, wuq_ref, wukv_ref,
                   qg_ref, kg_ref, q_ref, k_ref, v_ref, *, tm):
    ang = pos_ref[...].astype(F32) * invf_ref[...]
    lane = lax.broadcasted_iota(jnp.int32, (tm, LANES), 1)
    cos = jnp.cos(ang)
    sin_signed = jnp.where(lane < LANES // 2, -1.0, 1.0) * jnp.sin(ang)
    even_head = (lane // (QK_ROPE // 2)) % 2 == 0

    def rope(col):
        return col * cos + pltpu.roll(col, LANES // 2, axis=1) * sin_signed

    qg = qg_ref[...]
    kg = kg_ref[...]
    qscale = QK_HEAD ** -0.5 * LOG2E

    q = _dot(qln_ref[...], wuq_ref[...])
    for j in range(MLA_HEADS // 2):
        rc = q[:, MLA_DIM + j * LANES:MLA_DIM + (j + 1) * LANES]
        rc2 = rc * rc
        ss_even = jnp.sum(jnp.where(even_head, rc2, 0.0), axis=-1, keepdims=True)
        ss_odd = jnp.sum(jnp.where(even_head, 0.0, rc2), axis=-1, keepdims=True)
        r = []
        for h, ss_rope in ((2 * j, ss_even), (2 * j + 1, ss_odd)):
            qn = q[:, h * LANES:(h + 1) * LANES]
            ss = jnp.sum(qn * qn, axis=-1, keepdims=True) + ss_rope
            rh = lax.rsqrt(ss * (1.0 / QK_HEAD) + EPS) * qscale
            r.append(rh)
            q_ref[0, h, :, 0:LANES] = (qn * rh * qg[:, 0:LANES]).astype(BF16)
        roped = rope(rc * jnp.where(even_head, r[0], r[1]) * qg[:, LANES:2 * LANES]).astype(BF16)
        q_ref[0, 2 * j, :, LANES:2 * LANES] = roped
        q_ref[0, 2 * j + 1, :, LANES:2 * LANES] = roped

    kv = _dot(kvln_ref[...], wukv_ref[...])
    kp = kpe_ref[...]
    ss_pe = 0.5 * jnp.sum(kp * kp, axis=-1, keepdims=True)
    kr = rope(kp * kg[:, LANES:2 * LANES])
    for h in range(MLA_HEADS):
        kn = kv[:, h * LANES:(h + 1) * LANES]
        ss = jnp.sum(kn * kn, axis=-1, keepdims=True) + ss_pe
        rh = lax.rsqrt(ss * (1.0 / QK_HEAD) + EPS)
        k_ref[0, h, :, 0:LANES] = (kn * rh * kg[:, 0:LANES]).astype(BF16)
        mine = even_head if h % 2 == 0 else jnp.logical_not(even_head)
        k_ref[0, h, :, LANES:2 * LANES] = jnp.where(mine, kr * rh, 0.0).astype(BF16)
        v_ref[0, h, :, 0:V_HEAD] = kv[:, MLA_DIM + h * LANES:MLA_DIM + (h + 1) * LANES].astype(BF16)
        v_ref[0, h, :, V_HEAD:V_EXT] = jnp.ones((tm, V_EXT - V_HEAD), BF16)


def _upproj(qln, kvln, kpe, pos_col, invf, wuq_p, wukv_p, qg_p, kg_p, B, S, tm):
    T = qln.shape[0]
    tps = S // tm
    row = lambda n: pl.BlockSpec((tm, n), lambda i: (i, 0))
    head = lambda n: pl.BlockSpec((1, MLA_HEADS, tm, n), lambda i: (i // tps, 0, i % tps, 0))
    return pl.pallas_call(
        functools.partial(_upproj_kernel, tm=tm),
        grid=(T // tm,),
        in_specs=[row(Q_LORA), row(KV_LORA), row(LANES), row(1),
                  _resident((1, LANES)),
                  _resident(wuq_p.shape), _resident(wukv_p.shape),
                  _resident((1, 2 * LANES)), _resident((1, 2 * LANES))],
        out_specs=[head(QK_PAD), head(QK_PAD), head(V_EXT)],
        out_shape=[jax.ShapeDtypeStruct((B, MLA_HEADS, S, QK_PAD), BF16),
                   jax.ShapeDtypeStruct((B, MLA_HEADS, S, QK_PAD), BF16),
                   jax.ShapeDtypeStruct((B, MLA_HEADS, S, V_EXT), BF16)],
        compiler_params=_cparams(("arbitrary",)),
        name="upproj",
    )(qln, kvln, kpe, pos_col, invf, wuq_p, wukv_p, qg_p, kg_p)


def _attn_kernel(q_ref, k_ref, v_ref, g_ref, o_ref, m_scr, acc_scr, *, ts, nsub, tk):
    qi = pl.program_id(1)
    tq = ts * nsub
    for a in range(nsub):
        m_scr[a] = jnp.full((ts, LANES), -jnp.inf, F32)
        acc_scr[a] = jnp.zeros((ts, V_EXT), F32)

    def step(a, start, width, masked):
        q = q_ref[0, a * ts:(a + 1) * ts, :]
        k = k_ref[0, pl.ds(start, width), :]
        v = v_ref[0, pl.ds(start, width), :]
        s = _dot_nt(q, k)
        if masked:
            row = lax.broadcasted_iota(jnp.int32, (ts, width), 0) + (qi * tq + a * ts)
            col = lax.broadcasted_iota(jnp.int32, (ts, width), 1) + start
            s = jnp.where(col <= row, s, -jnp.inf)
        m_prev = m_scr[a]
        m_new = jnp.maximum(m_prev, jnp.max(s, axis=-1, keepdims=True))
        p = jnp.exp2(s - jnp.concatenate([m_new] * (width // LANES), axis=1))
        corr = jnp.exp2(m_prev - m_new)
        acc_scr[a] = (jnp.concatenate([corr] * (V_EXT // LANES), axis=1) * acc_scr[a]
                      + _dot(p.astype(BF16), v))
        m_scr[a] = m_new

    def body(j, carry):
        start = pl.multiple_of(j * tk, tk)
        for a in range(nsub):
            step(a, start, tk, False)
        return carry

    lax.fori_loop(0, qi * (tq // tk), body, 0)
    for a in range(nsub):
        for d in range(a + 1):
            step(a, pl.multiple_of(qi * tq + d * ts, ts), ts, d == a)

    for a in range(nsub):
        acc = acc_scr[a]
        o = acc[:, 0:V_HEAD] / acc[:, V_HEAD:V_EXT]
        o_ref[0, a * ts:(a + 1) * ts, :] = (o * _rms(o, V_HEAD) * g_ref[0]).astype(BF16)


def _attention(q, k, v, mla_g, B, S, ts, nsub, tk):
    BH = B * MLA_HEADS
    H = MLA_HEADS
    tq = ts * nsub
    return pl.pallas_call(
        functools.partial(_attn_kernel, ts=ts, nsub=nsub, tk=tk),
        grid=(BH, S // tq),
        in_specs=[pl.BlockSpec((1, tq, QK_PAD), lambda bh, qi: (bh, qi, 0)),
                  pl.BlockSpec((1, S, QK_PAD), lambda bh, qi: (bh, 0, 0)),
                  pl.BlockSpec((1, S, V_EXT), lambda bh, qi: (bh, 0, 0)),
                  pl.BlockSpec((1, 1, V_HEAD), lambda bh, qi: (bh % H, 0, 0))],
        out_specs=pl.BlockSpec((1, tq, V_HEAD), lambda bh, qi: (bh // H, qi, bh % H)),
        out_shape=jax.ShapeDtypeStruct((B, S, MLA_DIM), BF16),
        scratch_shapes=[pltpu.VMEM((nsub, ts, LANES), F32), pltpu.VMEM((nsub, ts, V_EXT), F32)],
        compiler_params=_cparams(("arbitrary", "arbitrary")),
        name="attn",
    )(q.reshape(BH, S, QK_PAD), k.reshape(BH, S, QK_PAD), v.reshape(BH, S, V_EXT),
      mla_g.reshape(H, 1, V_HEAD))


def _outproj_kernel(mc_ref, mm_ref, x_ref, mod_ref, w_ref, g_ref, rw_ref, rb_ref,
                    x1_ref, h2_ref, idx_ref, gate_ref, *, tm):
    mod = mod_ref[0]
    g_m, sh_f, sc_f = mod[2:3], mod[3:4], mod[4:5]
    mix = _dot(mc_ref[...], w_ref[0:CONV_DIM, :]) + _dot(mm_ref[...], w_ref[CONV_DIM:, :])
    x1 = x_ref[...] + g_m * mix
    x1_ref[...] = x1
    h2 = (x1 * _rms(x1, D_MODEL) * g_ref[...]) * (1.0 + sc_f) + sh_f
    h2_ref[...] = h2

    h_hi, h_lo = _split_bf16(h2)
    r_hi, r_lo = _split_bf16(rw_ref[...])
    logits = _dot_nt(r_hi, h_hi) + _dot_nt(r_lo, h_hi) + _dot_nt(r_hi, h_lo) + rb_ref[...]
    eid = lax.broadcasted_iota(jnp.int32, (N_EXPERTS, tm), 0)
    vals = []
    for kk in range(TOP_K):
        mx = jnp.max(logits, axis=0, keepdims=True)
        idx = jnp.min(jnp.where(logits == mx, eid, N_EXPERTS), axis=0, keepdims=True)
        idx_ref[kk:kk + 1, :] = idx
        vals.append(mx)
        logits = jnp.where(eid == idx, -jnp.inf, logits)
    ex = [jnp.exp(vv - vals[0]) for vv in vals]
    denom = ex[0] + ex[1] + ex[2] + ex[3]
    for kk in range(TOP_K):
        gate_ref[kk:kk + 1, :] = ex[kk] / denom


def _outproj(mc, mm, x2, mod3, w_out_b, norm_g, rw_t, rb_col, S, tm):
    T, D = x2.shape
    tps = S // tm
    row = lambda n: pl.BlockSpec((tm, n), lambda i: (i, 0))
    colblk = lambda n: pl.BlockSpec((n, tm), lambda i: (0, i))
    return pl.pallas_call(
        functools.partial(_outproj_kernel, tm=tm),
        grid=(T // tm,),
        in_specs=[row(CONV_DIM), row(MLA_DIM), row(D),
                  pl.BlockSpec((1, 6, D), lambda i: (i // tps, 0, 0)),
                  _resident((D, D)), _resident((1, D)),
                  _resident((N_EXPERTS, D)), _resident((N_EXPERTS, 1))],
        out_specs=[row(D), row(D), colblk(TOP_K), colblk(TOP_K)],
        out_shape=[jax.ShapeDtypeStruct((T, D), F32),
                   jax.ShapeDtypeStruct((T, D), F32),
                   jax.ShapeDtypeStruct((TOP_K, T), jnp.int32),
                   jax.ShapeDtypeStruct((TOP_K, T), F32)],
        compiler_params=_cparams(("arbitrary",)),
        name="outproj",
    )(mc, mm, x2, mod3, w_out_b, norm_g, rw_t, rb_col)


def _row_copy(src_hbm, src_row, dst, sem):
    return pltpu.make_async_copy(src_hbm.at[pl.ds(src_row, 1)], dst, sem)


ROW_ISSUE_UNROLL = 8


def _start_rows(idx_at, n_rows, src_hbm, dst, sem):
    def issue(r, c):
        _row_copy(src_hbm, idx_at(r), dst.at[pl.ds(r, 1)], sem).start()
        return c

    lax.fori_loop(0, n_rows, issue, 0, unroll=ROW_ISSUE_UNROLL)


def _wait_rows(n_rows, src_hbm, dst, sem):
    pltpu.make_async_copy(src_hbm.at[pl.ds(0, n_rows)], dst, sem).wait()


W_CHUNK = 128


def _load_expert(e, wgu_hbm, wd_hbm, wgu_s, wd_s, stage_gu, stage_d, wsem):
    n_gu = wgu_s.shape[0] // W_CHUNK
    n_d = wd_s.shape[0] // W_CHUNK

    def gu_copy(c):
        return pltpu.make_async_copy(wgu_hbm.at[e, pl.ds(c * W_CHUNK, W_CHUNK)], stage_gu.at[c % 2], wsem.at[c % 2])

    def d_copy(c):
        return pltpu.make_async_copy(wd_hbm.at[e, pl.ds(c * W_CHUNK, W_CHUNK)], stage_d.at[c % 2], wsem.at[2 + c % 2])

    gu_copy(0).start()
    for c in range(n_gu):
        (gu_copy(c + 1) if c + 1 < n_gu else d_copy(0)).start()
        gu_copy(c).wait()
        wgu_s[c * W_CHUNK:(c + 1) * W_CHUNK, :] = stage_gu[c % 2].astype(BF16)
    for c in range(n_d):
        if c + 1 < n_d:
            d_copy(c + 1).start()
        d_copy(c).wait()
        wd_s[c * W_CHUNK:(c + 1) * W_CHUNK, :] = stage_d[c % 2].astype(BF16)


def _moe_kernel(te_ref, nu_ref, tok_ref, tok_next_ref, h_hbm, wgu_hbm, bgu_ref, wd_hbm, bd_ref,
                o_ref, xbuf, sem, wgu_s, wd_s, stage_gu, stage_d, wsem, *, tme, fc):
    i = pl.program_id(0)
    n_used = nu_ref[0]
    slot = i % 2

    @pl.when(i == 0)
    def _():
        _start_rows(lambda r: tok_ref[0, 0, r], tme, h_hbm, xbuf.at[0], sem.at[0])

    e_cur = te_ref[i]
    changed = jnp.logical_or(i == 0, e_cur != te_ref[jnp.maximum(i - 1, 0)])

    @pl.when(jnp.logical_and(i < n_used, changed))
    def _():
        _load_expert(e_cur, wgu_hbm, wd_hbm, wgu_s, wd_s, stage_gu, stage_d, wsem)

    @pl.when(i < n_used)
    def _():
        _wait_rows(tme, h_hbm, xbuf.at[slot], sem.at[slot])
        x = xbuf[slot].astype(BF16)
        acts = []
        n_chunks = D_EXPERT // fc
        per = tme // n_chunks
        for c in range(n_chunks):
            for r in range(c * per, (c + 1) * per):
                _row_copy(h_hbm, tok_next_ref[0, 0, r], xbuf.at[1 - slot, pl.ds(r, 1)], sem.at[1 - slot]).start()
            g = _dot(x, wgu_s[:, c * fc:(c + 1) * fc]) + bgu_ref[0, :, c * fc:(c + 1) * fc]
            u = (_dot(x, wgu_s[:, D_EXPERT + c * fc:D_EXPERT + (c + 1) * fc])
                 + bgu_ref[0, :, D_EXPERT + c * fc:D_EXPERT + (c + 1) * fc])
            g = jnp.minimum(g, SWIGLU_LIMIT)
            u = jnp.clip(u, -SWIGLU_LIMIT, SWIGLU_LIMIT)
            acts.append(((u + 1.0) * g * jax.nn.sigmoid(SWIGLU_ALPHA * g)).astype(BF16))
        o_ref[...] = _dot(jnp.concatenate(acts, axis=1), wd_s[...]) + bd_ref[0]

    @pl.when(i == n_used)
    def _():
        _wait_rows(tme, h_hbm, xbuf.at[slot], sem.at[slot])

    @pl.when(i >= n_used)
    def _():
        o_ref[...] = jnp.zeros(o_ref.shape, F32)


def _moe(tile_expert, n_used, buf_tok, h2, w_gu, b_gu, w_d, b_d, tme):
    n_tiles = tile_expert.shape[0]
    D = h2.shape[1]
    tok3 = buf_tok.reshape(n_tiles, 1, tme)
    grid_spec = pltpu.PrefetchScalarGridSpec(
        num_scalar_prefetch=2,
        grid=(n_tiles,),
        in_specs=[
            pl.BlockSpec((1, 1, tme), lambda i, te, nu: (i, 0, 0), memory_space=pltpu.SMEM),
            pl.BlockSpec((1, 1, tme), lambda i, te, nu: (jnp.minimum(i + 1, n_tiles - 1), 0, 0),
                         memory_space=pltpu.SMEM),
            pl.BlockSpec(memory_space=pl.ANY),
            pl.BlockSpec(memory_space=pl.ANY),
            pl.BlockSpec((1, 1, 2 * D_EXPERT), lambda i, te, nu: (te[i], 0, 0)),
            pl.BlockSpec(memory_space=pl.ANY),
            pl.BlockSpec((1, 1, D), lambda i, te, nu: (te[i], 0, 0)),
        ],
        out_specs=pl.BlockSpec((tme, D), lambda i, te, nu: (i, 0)),
        scratch_shapes=[pltpu.VMEM((2, tme, D), F32), pltpu.SemaphoreType.DMA((2,)),
                        pltpu.VMEM((D, 2 * D_EXPERT), BF16), pltpu.VMEM((D_EXPERT, D), BF16),
                        pltpu.VMEM((2, W_CHUNK, 2 * D_EXPERT), F32), pltpu.VMEM((2, W_CHUNK, D), F32),
                        pltpu.SemaphoreType.DMA((4,))],
    )
    return pl.pallas_call(
        functools.partial(_moe_kernel, tme=tme, fc=512),
        grid_spec=grid_spec,
        out_shape=jax.ShapeDtypeStruct((n_tiles * tme, D), F32),
        compiler_params=_cparams(("arbitrary",)),
        name="moe",
    )(tile_expert, n_used, tok3, tok3, h2,
      w_gu, b_gu.reshape(N_EXPERTS, 1, 2 * D_EXPERT), w_d, b_d.reshape(N_EXPERTS, 1, D))


def _combine_kernel(dest_ref, dest_next_ref, gate_ref, x1_ref, mod_ref, y_hbm, o_ref, ybuf, sem, *, tm, n_tiles):
    i = pl.program_id(0)
    slot = i % 2

    @pl.when(i == 0)
    def _():
        _start_rows(lambda r: dest_ref[0, 0, r], TOP_K * tm, y_hbm, ybuf.at[0], sem.at[0])

    @pl.when(i + 1 < n_tiles)
    def _():
        _start_rows(lambda r: dest_next_ref[0, 0, r], TOP_K * tm, y_hbm, ybuf.at[1 - slot], sem.at[1 - slot])

    _wait_rows(TOP_K * tm, y_hbm, ybuf.at[slot], sem.at[slot])
    gates = gate_ref[...]
    y = gates[:, 0:1] * ybuf[slot, 0:tm, :]
    for kk in range(1, TOP_K):
        y = y + gates[:, kk:kk + 1] * ybuf[slot, kk * tm:(kk + 1) * tm, :]
    g_f = mod_ref[0][5:6]
    o_ref[...] = x1_ref[...] + g_f * y


def _combine(dest_tiles, gates_t, x1, mod3, y_sorted, S, tm):
    T, D = x1.shape
    tps = S // tm
    n_tiles = T // tm
    return pl.pallas_call(
        functools.partial(_combine_kernel, tm=tm, n_tiles=n_tiles),
        grid=(n_tiles,),
        in_specs=[pl.BlockSpec((1, 1, TOP_K * tm), lambda i: (i, 0, 0), memory_space=pltpu.SMEM),
                  pl.BlockSpec((1, 1, TOP_K * tm), lambda i: (jnp.minimum(i + 1, n_tiles - 1), 0, 0),
                               memory_space=pltpu.SMEM),
                  pl.BlockSpec((tm, TOP_K), lambda i: (i, 0)),
                  pl.BlockSpec((tm, D), lambda i: (i, 0)),
                  pl.BlockSpec((1, 6, D), lambda i: (i // tps, 0, 0)),
                  pl.BlockSpec(memory_space=pl.ANY)],
        out_specs=pl.BlockSpec((tm, D), lambda i: (i, 0)),
        out_shape=jax.ShapeDtypeStruct((T, D), F32),
        scratch_shapes=[pltpu.VMEM((2, TOP_K * tm, D), F32), pltpu.SemaphoreType.DMA((2,))],
        compiler_params=_cparams(("arbitrary",)),
        name="combine",
    )(dest_tiles, dest_tiles, gates_t, x1, mod3, y_sorted)


def _route(top_i, tme):
    T = top_i.shape[1]
    TK = T * TOP_K
    assert TK % tme == 0
    flat_e = top_i.T.reshape(-1)
    is_e = flat_e[:, None] == jnp.arange(N_EXPERTS, dtype=jnp.int32)[None, :]
    onehot = is_e.astype(jnp.int32)
    blk = 512 if TK % 512 == 0 else TK
    oh_b = is_e.astype(BF16).reshape(TK // blk, blk, N_EXPERTS)
    tri = (jnp.arange(blk)[:, None] >= jnp.arange(blk)[None, :]).astype(BF16)
    within = jnp.einsum('ij,bjk->bik', tri, oh_b, preferred_element_type=F32).astype(jnp.int32)
    blk_tot = within[:, -1, :]
    csum = (within + (jnp.cumsum(blk_tot, axis=0) - blk_tot)[:, None, :]).reshape(TK, N_EXPERTS)
    counts = csum[-1]
    padded = ((counts + tme - 1) // tme) * tme
    padded_end = jnp.cumsum(padded)
    padded_start = padded_end - padded
    group_start = jnp.cumsum(counts) - counts
    dest_orig = jnp.sum(onehot * (padded_start[None, :] + csum - onehot), axis=1).astype(jnp.int32)

    n_tiles = -(-TK // tme) + N_EXPERTS
    tile_start = jnp.arange(n_tiles) * tme
    tile_expert = jnp.minimum(jnp.sum(tile_start[:, None] >= padded_end[None, :], axis=1),
                              N_EXPERTS - 1).astype(jnp.int32)
    n_used = (padded_end[-1] // tme).astype(jnp.int32).reshape(1)

    tok_sorted = (jnp.argsort(flat_e) // TOP_K).astype(jnp.int32)
    r = (tile_start - padded_start[tile_expert])[:, None] + jnp.arange(tme)[None, :]
    valid = r < counts[tile_expert][:, None]
    src = jnp.clip(group_start[tile_expert][:, None] + r, 0, TK - 1)
    buf_tok = jnp.where(valid, tok_sorted[src.reshape(-1)].reshape(n_tiles, tme), 0).reshape(-1)
    return tile_expert, n_used, buf_tok, dest_orig


def _permute_weights(w_in, w_uq, w_ukv, q_head_g, k_head_g):
    half = QK_ROPE // 2
    pe0 = _C_PE
    pe_cols = jnp.concatenate([jnp.arange(pe0, pe0 + half), jnp.arange(pe0, pe0 + half),
                               jnp.arange(pe0 + half, pe0 + 2 * half), jnp.arange(pe0 + half, pe0 + 2 * half)])
    w_in_p = jnp.concatenate([w_in[:, :pe0], w_in[:, pe_cols]], axis=1).astype(BF16)

    hq = jnp.arange(MLA_HEADS)[:, None] * QK_HEAD
    nope_cols = (hq + jnp.arange(QK_NOPE)[None, :]).reshape(-1)
    rope_cols = []
    for j in range(MLA_HEADS // 2):
        for part in range(2):
            for h in (2 * j, 2 * j + 1):
                rope_cols.append(h * QK_HEAD + QK_NOPE + part * half + jnp.arange(half))
    wuq_p = w_uq[:, jnp.concatenate([nope_cols] + rope_cols)].astype(BF16)

    hk = jnp.arange(MLA_HEADS)[:, None] * (QK_NOPE + V_HEAD)
    kn_cols = (hk + jnp.arange(QK_NOPE)[None, :]).reshape(-1)
    v_cols = (hk + QK_NOPE + jnp.arange(V_HEAD)[None, :]).reshape(-1)
    wukv_p = w_ukv[:, jnp.concatenate([kn_cols, v_cols])].astype(BF16)

    def gain_p(g):
        x1, x2 = g[QK_NOPE:QK_NOPE + half], g[QK_NOPE + half:]
        return jnp.concatenate([g[:QK_NOPE], x1, x1, x2, x2]).reshape(1, 2 * LANES)

    return w_in_p, wuq_p, wukv_p, gain_p(q_head_g), gain_p(k_head_g)


def _tile(n, pref):
    t = pref
    while n % t:
        t //= 2
    return t


def kernel(x, c, positions, w_ada, b_ada, norm_mix_g, w_in, conv_w, q_lat_norm_g, w_uq, kv_lat_norm_g, w_ukv, q_head_norm_g, k_head_norm_g, conv_out_norm_g, mla_out_norm_g, w_out, norm_ffn_g, router_w, router_b, w_gate_up, b_gate_up, w_down, b_down):
    B, S, D = x.shape
    T = B * S
    assert D == D_MODEL and w_ada.shape[0] == 1 and S % 128 == 0
    l = 0
    tm = _tile(S, 512)
    ts = 512 if S >= 8192 else S // 8
    tme = 512 if T * TOP_K >= 32768 else 256
    tmc = _tile(S, 256)

    mod3 = _ada(c, w_ada[l], b_ada[l]).reshape(B, 6, D)
    x2 = x.reshape(T, D)

    w_in_p, wuq_p, wukv_p, qg_p, kg_p = _permute_weights(
        w_in[l], w_uq[l], w_ukv[l], q_head_norm_g[l], k_head_norm_g[l])
    half = QK_ROPE // 2
    inv_freq = ROPE_THETA ** (-jnp.arange(half, dtype=F32) / half)
    invf = jnp.tile(inv_freq, 4).reshape(1, LANES)

    mc, qln, kvln, kpe = _inproj(
        x2, mod3, norm_mix_g[l].reshape(1, D), w_in_p, conv_w[l],
        conv_out_norm_g[l].reshape(1, CONV_DIM), q_lat_norm_g[l].reshape(1, Q_LORA),
        kv_lat_norm_g[l].reshape(1, KV_LORA), S, tm)
    q, k, v = _upproj(qln, kvln, kpe, positions.reshape(T, 1), invf, wuq_p, wukv_p, qg_p, kg_p, B, S, tm)
    mm = _attention(q, k, v, mla_out_norm_g[l], B, S, ts, 4, 4 * ts).reshape(T, MLA_DIM)
    x1, h2, top_i, gates = _outproj(
        mc, mm, x2, mod3, w_out[l].astype(BF16), norm_ffn_g[l].reshape(1, D),
        router_w[l].T, router_b[l].reshape(N_EXPERTS, 1), S, tm)

    tile_expert, n_used, buf_tok, dest_orig = _route(top_i, tme)
    y_sorted = _moe(tile_expert, n_used, buf_tok, h2, w_gate_up[l], b_gate_up[l], w_down[l], b_down[l], tme)
    dest_tiles = dest_orig.reshape(T // tmc, tmc, TOP_K).transpose(0, 2, 1).reshape(T // tmc, 1, TOP_K * tmc)
    out = _combine(dest_tiles, gates.T, x1, mod3, y_sorted, S, tmc)
    return out.reshape(B, S, D)
```

```python
import functools

import jax
import jax.numpy as jnp
from jax import lax
from jax.experimental import pallas as pl
from jax.experimental.pallas import tpu as pltpu

F32 = jnp.float32
BF16 = jnp.bfloat16

D_MODEL = 2048
CONV_GROUPS = 8
CONV_DIM = 1024
CONV_K = 3
MLA_HEADS = 8
QK_NOPE = 128
QK_ROPE = 64
QK_HEAD = QK_NOPE + QK_ROPE
V_HEAD = 128
MLA_DIM = MLA_HEADS * V_HEAD
Q_LORA = 768
KV_LORA = 512
ROPE_THETA = 10000.0
N_EXPERTS = 32
TOP_K = 4
D_EXPERT = D_MODEL
SWIGLU_LIMIT = 7.0
SWIGLU_ALPHA = 1.702
EPS = 1e-6

LANES = 128
QK_PAD = 2 * LANES
V_EXT = 2 * LANES
LOG2E = 1.4426950408889634
_C_B, _C_C, _C_U = 0, CONV_DIM, 2 * CONV_DIM
_C_Q = 3 * CONV_DIM
_C_KV = _C_Q + Q_LORA
_C_PE = _C_KV + KV_LORA
D_IN_PAD = _C_PE + LANES
VMEM_LIMIT = 56 * 1024 * 1024


def _cparams(sem):
    return pltpu.CompilerParams(dimension_semantics=sem, vmem_limit_bytes=VMEM_LIMIT)


def _resident(shape):
    nd = len(shape)
    return pl.BlockSpec(shape, lambda *_: (0,) * nd, pipeline_mode=pl.Buffered(1))


def _split_bf16(a):
    hi = a.astype(BF16)
    lo = (a - hi.astype(F32)).astype(BF16)
    return hi, lo


def _dot(a, b):
    return jnp.dot(a, b, preferred_element_type=F32)


def _dot_nt(a, b):
    return lax.dot_general(a, b, (((1,), (1,)), ((), ())), preferred_element_type=F32)


def _rms(x, n):
    return lax.rsqrt(jnp.sum(x * x, axis=-1, keepdims=True) * (1.0 / n) + EPS)


def _ada_kernel(c_ref, w_ref, b_ref, o_ref):
    c = c_ref[...]
    cond = c * jax.nn.sigmoid(c)
    c_hi, c_lo = _split_bf16(cond)
    w_hi, w_lo = _split_bf16(w_ref[...])
    o_ref[...] = _dot(c_hi, w_hi) + _dot(c_hi, w_lo) + _dot(c_lo, w_hi) + b_ref[...]


def _ada(c, w_ada, b_ada):
    B, D = c.shape
    n_out = w_ada.shape[1]
    rows = 8
    c_pad = jnp.zeros((rows, D), F32).at[:B].set(c)
    tn = 1024
    out = pl.pallas_call(
        _ada_kernel,
        grid=(n_out // tn,),
        in_specs=[pl.BlockSpec((rows, D), lambda j: (0, 0)),
                  pl.BlockSpec((D, tn), lambda j: (0, j)),
                  pl.BlockSpec((1, tn), lambda j: (0, j))],
        out_specs=pl.BlockSpec((rows, tn), lambda j: (0, j)),
        out_shape=jax.ShapeDtypeStruct((rows, n_out), F32),
        compiler_params=_cparams(("arbitrary",)),
        name="ada",
    )(c_pad, w_ada, b_ada.reshape(1, n_out))
    return out[:B]


def _inproj_kernel(x_ref, mod_ref, g_ref, w_ref, cw_ref, cg_ref, qg_ref, kvg_ref,
                   mc_ref, qln_ref, kvln_ref, kpe_ref, cu_scr, *, tm, tiles_per_seq):
    i = pl.program_id(0)
    x = x_ref[...]
    mod = mod_ref[0]
    sh, sc = mod[0:1], mod[1:2]
    h = (x * _rms(x, D_MODEL) * g_ref[...]) * (1.0 + sc) + sh
    hb = h.astype(BF16)

    def proj(lo, hi):
        return _dot(hb, w_ref[:, lo:hi])

    cu = proj(_C_C, _C_U) * proj(_C_U, _C_Q)

    @pl.when(i % tiles_per_seq == 0)
    def _():
        cu_scr[0:8, :] = jnp.zeros((8, CONV_DIM), F32)

    cu_scr[8:8 + tm, :] = cu
    s1 = cu_scr[7:7 + tm, :]
    s2 = cu_scr[6:6 + tm, :]
    cw = cw_ref[...]
    conv = cw[0:1] * s2 + cw[1:2] * s1 + cw[2:3] * cu
    cu_scr[0:8, :] = cu_scr[tm:tm + 8, :]
    y = proj(_C_B, _C_C) * conv
    cg = cg_ref[...]
    for g in range(CONV_GROUPS):
        sl = slice(g * LANES, (g + 1) * LANES)
        yg = y[:, sl]
        mc_ref[:, sl] = (yg * _rms(yg, LANES) * cg[:, sl]).astype(BF16)

    ql = proj(_C_Q, _C_KV)
    qln_ref[...] = (ql * _rms(ql, Q_LORA) * qg_ref[...]).astype(BF16)
    kvl = proj(_C_KV, _C_PE)
    kvln_ref[...] = (kvl * _rms(kvl, KV_LORA) * kvg_ref[...]).astype(BF16)
    kpe_ref[...] = proj(_C_PE, D_IN_PAD)


def _inproj(x2, mod3, norm_g, w_in_p, conv_w, conv_g, q_lat_g, kv_lat_g, S, tm):
    T, D = x2.shape
    tps = S // tm
    row = lambda n: pl.BlockSpec((tm, n), lambda i: (i, 0))
    return pl.pallas_call(
        functools.partial(_inproj_kernel, tm=tm, tiles_per_seq=tps),
        grid=(T // tm,),
        in_specs=[row(D),
                  pl.BlockSpec((1, 6, D), lambda i: (i // tps, 0, 0)),
                  _resident((1, D)),
                  _resident((D, D_IN_PAD)),
                  _resident((CONV_K, CONV_DIM)),
                  _resident((1, CONV_DIM)),
                  _resident((1, Q_LORA)),
                  _resident((1, KV_LORA))],
        out_specs=[row(CONV_DIM), row(Q_LORA), row(KV_LORA), row(LANES)],
        out_shape=[jax.ShapeDtypeStruct((T, CONV_DIM), BF16),
                   jax.ShapeDtypeStruct((T, Q_LORA), BF16),
                   jax.ShapeDtypeStruct((T, KV_LORA), BF16),
                   jax.ShapeDtypeStruct((T, LANES), F32)],
        scratch_shapes=[pltpu.VMEM((tm + 8, CONV_DIM), F32)],
        compiler_params=_cparams(("arbitrary",)),
        name="inproj",
    )(x2, mod3, norm_g, w_in_p, conv_w, conv_g, q_lat_g, kv_lat_g)


def _upproj_kernel(qln_ref, kvln_ref, kpe_ref, pos_ref, invf_ref, wuq_ref, wukv_ref,
                   qg_ref, kg_ref, q_ref, k_ref, v_ref, *, tm):
    ang = pos_ref[...].astype(F32) * invf_ref[...]
    lane = lax.broadcasted_iota(jnp.int32, (tm, LANES), 1)
    cos = jnp.cos(ang)
    sin_signed = jnp.where(lane < LANES // 2, -1.0, 1.0) * jnp.sin(ang)
    even_head = (lane // (QK_ROPE // 2)) % 2 == 0

    def rope(col):
        return col * cos + pltpu.roll(col, LANES // 2, axis=1) * sin_signed

    qg = qg_ref[...]
    kg = kg_ref[...]
    qscale = QK_HEAD ** -0.5 * LOG2E

    q = _dot(qln_ref[...], wuq_ref[...])
    for j in range(MLA_HEADS // 2):
        rc = q[:, MLA_DIM + j * LANES:MLA_DIM + (j + 1) * LANES]
        rc2 = rc * rc
        ss_even = jnp.sum(jnp.where(even_head, rc2, 0.0), axis=-1, keepdims=True)
        ss_odd = jnp.sum(jnp.where(even_head, 0.0, rc2), axis=-1, keepdims=True)
        r = []
        for h, ss_rope in ((2 * j, ss_even), (2 * j + 1, ss_odd)):
            qn = q[:, h * LANES:(h + 1) * LANES]
            ss = jnp.sum(qn * qn, axis=-1, keepdims=True) + ss_rope
            rh = lax.rsqrt(ss * (1.0 / QK_HEAD) + EPS) * qscale
            r.append(rh)
            q_ref[0, h, :, 0:LANES] = (qn * rh * qg[:, 0:LANES]).astype(BF16)
        roped = rope(rc * jnp.where(even_head, r[0], r[1]) * qg[:, LANES:2 * LANES]).astype(BF16)
        q_ref[0, 2 * j, :, LANES:2 * LANES] = roped
        q_ref[0, 2 * j + 1, :, LANES:2 * LANES] = roped

    kv = _dot(kvln_ref[...], wukv_ref[...])
    kp = kpe_ref[...]
    ss_pe = 0.5 * jnp.sum(kp * kp, axis=-1, keepdims=True)
    kr = rope(kp * kg[:, LANES:2 * LANES])
    for h in range(MLA_HEADS):
        kn = kv[:, h * LANES:(h + 1) * LANES]
        ss = jnp.sum(kn * kn, axis=-1, keepdims=True) + ss_pe
        rh = lax.rsqrt(ss * (1.0 / QK_HEAD) + EPS)
        k_ref[0, h, :, 0:LANES] = (kn * rh * kg[:, 0:LANES]).astype(BF16)
        mine = even_head if h % 2 == 0 else jnp.logical_not(even_head)
        k_ref[0, h, :, LANES:2 * LANES] = jnp.where(mine, kr * rh, 0.0).astype(BF16)
        v_ref[0, h, :, 0:V_HEAD] = kv[:, MLA_DIM + h * LANES:MLA_DIM + (h + 1) * LANES].astype(BF16)
        v_ref[0, h, :, V_HEAD:V_EXT] = jnp.ones((tm, V_EXT - V_HEAD), BF16)


def _upproj(qln, kvln, kpe, pos_col, invf, wuq_p, wukv_p, qg_p, kg_p, B, S, tm):
    T = qln.shape[0]
    tps = S // tm
    row = lambda n: pl.BlockSpec((tm, n), lambda i: (i, 0))
    head = lambda n: pl.BlockSpec((1, MLA_HEADS, tm, n), lambda i: (i // tps, 0, i % tps, 0))
    return pl.pallas_call(
        functools.partial(_upproj_kernel, tm=tm),
        grid=(T // tm,),
        in_specs=[row(Q_LORA), row(KV_LORA), row(LANES), row(1),
                  _resident((1, LANES)),
                  _resident(wuq_p.shape), _resident(wukv_p.shape),
                  _resident((1, 2 * LANES)), _resident((1, 2 * LANES))],
        out_specs=[head(QK_PAD), head(QK_PAD), head(V_EXT)],
        out_shape=[jax.ShapeDtypeStruct((B, MLA_HEADS, S, QK_PAD), BF16),
                   jax.ShapeDtypeStruct((B, MLA_HEADS, S, QK_PAD), BF16),
                   jax.ShapeDtypeStruct((B, MLA_HEADS, S, V_EXT), BF16)],
        compiler_params=_cparams(("arbitrary",)),
        name="upproj",
    )(qln, kvln, kpe, pos_col, invf, wuq_p, wukv_p, qg_p, kg_p)


def _attn_kernel(q_ref, k_ref, v_ref, g_ref, o_ref, m_scr, acc_scr, *, ts, nsub, tk):
    qi = pl.program_id(1)
    tq = ts * nsub
    for a in range(nsub):
        m_scr[a] = jnp.full((ts, LANES), -jnp.inf, F32)
        acc_scr[a] = jnp.zeros((ts, V_EXT), F32)

    def step(a, start, width, masked):
        q = q_ref[0, a * ts:(a + 1) * ts, :]
        k = k_ref[0, pl.ds(start, width), :]
        v = v_ref[0, pl.ds(start, width), :]
        s = _dot_nt(q, k)
        if masked:
            row = lax.broadcasted_iota(jnp.int32, (ts, width), 0) + (qi * tq + a * ts)
            col = lax.broadcasted_iota(jnp.int32, (ts, width), 1) + start
            s = jnp.where(col <= row, s, -jnp.inf)
        m_prev = m_scr[a]
        m_new = jnp.maximum(m_prev, jnp.max(s, axis=-1, keepdims=True))
        p = jnp.exp2(s - jnp.concatenate([m_new] * (width // LANES), axis=1))
        corr = jnp.exp2(m_prev - m_new)
        acc_scr[a] = (jnp.concatenate([corr] * (V_EXT // LANES), axis=1) * acc_scr[a]
                      + _dot(p.astype(BF16), v))
        m_scr[a] = m_new

    def body(j, carry):
        start = pl.multiple_of(j * tk, tk)
        for a in range(nsub):
            step(a, start, tk, False)
        return carry

    lax.fori_loop(0, qi * (tq // tk), body, 0)
    for a in range(nsub):
        for d in range(a + 1):
            step(a, pl.multiple_of(qi * tq + d * ts, ts), ts, d == a)

    for a in range(nsub):
        acc = acc_scr[a]
        o = acc[:, 0:V_HEAD] / acc[:, V_HEAD:V_EXT]
        o_ref[0, a * ts:(a + 1) * ts, :] = (o * _rms(o, V_HEAD) * g_ref[0]).astype(BF16)


def _attention(q, k, v, mla_g, B, S, ts, nsub, tk):
    BH = B * MLA_HEADS
    H = MLA_HEADS
    tq = ts * nsub
    return pl.pallas_call(
        functools.partial(_attn_kernel, ts=ts, nsub=nsub, tk=tk),
        grid=(BH, S // tq),
        in_specs=[pl.BlockSpec((1, tq, QK_PAD), lambda bh, qi: (bh, qi, 0)),
                  pl.BlockSpec((1, S, QK_PAD), lambda bh, qi: (bh, 0, 0)),
                  pl.BlockSpec((1, S, V_EXT), lambda bh, qi: (bh, 0, 0)),
                  pl.BlockSpec((1, 1, V_HEAD), lambda bh, qi: (bh % H, 0, 0))],
        out_specs=pl.BlockSpec((1, tq, V_HEAD), lambda bh, qi: (bh // H, qi, bh % H)),
        out_shape=jax.ShapeDtypeStruct((B, S, MLA_DIM), BF16),
        scratch_shapes=[pltpu.VMEM((nsub, ts, LANES), F32), pltpu.VMEM((nsub, ts, V_EXT), F32)],
        compiler_params=_cparams(("arbitrary", "arbitrary")),
        name="attn",
    )(q.reshape(BH, S, QK_PAD), k.reshape(BH, S, QK_PAD), v.reshape(BH, S, V_EXT),
      mla_g.reshape(H, 1, V_HEAD))


def _outproj_kernel(mc_ref, mm_ref, x_ref, mod_ref, w_ref, g_ref, rw_ref, rb_ref,
                    x1_ref, h2_ref, idx_ref, gate_ref, *, tm):
    mod = mod_ref[0]
    g_m, sh_f, sc_f = mod[2:3], mod[3:4], mod[4:5]
    mix = _dot(mc_ref[...], w_ref[0:CONV_DIM, :]) + _dot(mm_ref[...], w_ref[CONV_DIM:, :])
    x1 = x_ref[...] + g_m * mix
    x1_ref[...] = x1
    h2 = (x1 * _rms(x1, D_MODEL) * g_ref[...]) * (1.0 + sc_f) + sh_f
    h2_ref[...] = h2

    h_hi, h_lo = _split_bf16(h2)
    r_hi, r_lo = _split_bf16(rw_ref[...])
    logits = _dot_nt(r_hi, h_hi) + _dot_nt(r_lo, h_hi) + _dot_nt(r_hi, h_lo) + rb_ref[...]
    eid = lax.broadcasted_iota(jnp.int32, (N_EXPERTS, tm), 0)
    vals = []
    for kk in range(TOP_K):
        mx = jnp.max(logits, axis=0, keepdims=True)
        idx = jnp.min(jnp.where(logits == mx, eid, N_EXPERTS), axis=0, keepdims=True)
        idx_ref[kk:kk + 1, :] = idx
        vals.append(mx)
        logits = jnp.where(eid == idx, -jnp.inf, logits)
    ex = [jnp.exp(vv - vals[0]) for vv in vals]
    denom = ex[0] + ex[1] + ex[2] + ex[3]
    for kk in range(TOP_K):
        gate_ref[kk:kk + 1, :] = ex[kk] / denom


def _outproj(mc, mm, x2, mod3, w_out_b, norm_g, rw_t, rb_col, S, tm):
    T, D = x2.shape
    tps = S // tm
    row = lambda n: pl.BlockSpec((tm, n), lambda i: (i, 0))
    colblk = lambda n: pl.BlockSpec((n, tm), lambda i: (0, i))
    return pl.pallas_call(
        functools.partial(_outproj_kernel, tm=tm),
        grid=(T // tm,),
        in_specs=[row(CONV_DIM), row(MLA_DIM), row(D),
                  pl.BlockSpec((1, 6, D), lambda i: (i // tps, 0, 0)),
                  _resident((D, D)), _resident((1, D)),
                  _resident((N_EXPERTS, D)), _resident((N_EXPERTS, 1))],
        out_specs=[row(D), row(D), colblk(TOP_K), colblk(TOP_K)],
        out_shape=[jax.ShapeDtypeStruct((T, D), F32),
                   jax.ShapeDtypeStruct((T, D), F32),
                   jax.ShapeDtypeStruct((TOP_K, T), jnp.int32),
                   jax.ShapeDtypeStruct((TOP_K, T), F32)],
        compiler_params=_cparams(("arbitrary",)),
        name="outproj",
    )(mc, mm, x2, mod3, w_out_b, norm_g, rw_t, rb_col)


def _row_copy(src_hbm, src_row, dst, sem):
    return pltpu.make_async_copy(src_hbm.at[pl.ds(src_row, 1)], dst, sem)


ROW_ISSUE_UNROLL = 8


def _start_rows(idx_at, n_rows, src_hbm, dst, sem):
    def issue(r, c):
        _row_copy(src_hbm, idx_at(r), dst.at[pl.ds(r, 1)], sem).start()
        return c

    lax.fori_loop(0, n_rows, issue, 0, unroll=ROW_ISSUE_UNROLL)


def _wait_rows(n_rows, src_hbm, dst, sem):
    pltpu.make_async_copy(src_hbm.at[pl.ds(0, n_rows)], dst, sem).wait()


W_CHUNK = 128


def _load_expert(e, wgu_hbm, wd_hbm, wgu_s, wd_s, stage_gu, stage_d, wsem):
    n_gu = wgu_s.shape[0] // W_CHUNK
    n_d = wd_s.shape[0] // W_CHUNK

    def gu_copy(c):
        return pltpu.make_async_copy(wgu_hbm.at[e, pl.ds(c * W_CHUNK, W_CHUNK)], stage_gu.at[c % 2], wsem.at[c % 2])

    def d_copy(c):
        return pltpu.make_async_copy(wd_hbm.at[e, pl.ds(c * W_CHUNK, W_CHUNK)], stage_d.at[c % 2], wsem.at[2 + c % 2])

    gu_copy(0).start()
    for c in range(n_gu):
        (gu_copy(c + 1) if c + 1 < n_gu else d_copy(0)).start()
        gu_copy(c).wait()
        wgu_s[c * W_CHUNK:(c + 1) * W_CHUNK, :] = stage_gu[c % 2].astype(BF16)
    for c in range(n_d):
        if c + 1 < n_d:
            d_copy(c + 1).start()
        d_copy(c).wait()
        wd_s[c * W_CHUNK:(c + 1) * W_CHUNK, :] = stage_d[c % 2].astype(BF16)


def _moe_kernel(te_ref, nu_ref, tok_ref, tok_next_ref, dst_prev_ref, h_hbm, wgu_hbm, bgu_ref, wd_hbm, bd_ref,
                y_hbm, xbuf, sem, wgu_s, wd_s, stage_gu, stage_d, wsem, obuf, osem, *, tme, fc):
    i = pl.program_id(0)
    n_used = nu_ref[0]
    slot = i % 2

    def send_row(r, src_slot):
        return pltpu.make_async_copy(obuf.at[src_slot, pl.ds(r, 1)],
                                     y_hbm.at[pl.ds(dst_prev_ref[0, 0, r], 1)], osem.at[src_slot])

    def wait_sent(src_slot):
        pltpu.make_async_copy(obuf.at[src_slot], y_hbm.at[pl.ds(0, tme)], osem.at[src_slot]).wait()

    @pl.when(i == 0)
    def _():
        _start_rows(lambda r: tok_ref[0, 0, r], tme, h_hbm, xbuf.at[0], sem.at[0])
        obuf[1] = jnp.zeros(obuf.shape[1:], F32)

    e_cur = te_ref[i]
    changed = jnp.logical_or(i == 0, e_cur != te_ref[jnp.maximum(i - 1, 0)])

    @pl.when(jnp.logical_and(i < n_used, changed))
    def _():
        _load_expert(e_cur, wgu_hbm, wd_hbm, wgu_s, wd_s, stage_gu, stage_d, wsem)

    @pl.when(jnp.logical_and(i >= 1, i <= n_used))
    def _():
        wait_sent(slot)

    @pl.when(i < n_used)
    def _():
        _wait_rows(tme, h_hbm, xbuf.at[slot], sem.at[slot])
        x = xbuf[slot].astype(BF16)
        acts = []
        n_chunks = D_EXPERT // fc
        per = tme // n_chunks
        for c in range(n_chunks):
            for r in range(c * per, (c + 1) * per):
                _row_copy(h_hbm, tok_next_ref[0, 0, r], xbuf.at[1 - slot, pl.ds(r, 1)], sem.at[1 - slot]).start()
                send_row(r, 1 - slot).start()
            g = _dot(x, wgu_s[:, c * fc:(c + 1) * fc]) + bgu_ref[0, :, c * fc:(c + 1) * fc]
            u = (_dot(x, wgu_s[:, D_EXPERT + c * fc:D_EXPERT + (c + 1) * fc])
                 + bgu_ref[0, :, D_EXPERT + c * fc:D_EXPERT + (c + 1) * fc])
            g = jnp.minimum(g, SWIGLU_LIMIT)
            u = jnp.clip(u, -SWIGLU_LIMIT, SWIGLU_LIMIT)
            acts.append(((u + 1.0) * g * jax.nn.sigmoid(SWIGLU_ALPHA * g)).astype(BF16))
        obuf[slot] = _dot(jnp.concatenate(acts, axis=1), wd_s[...]) + bd_ref[0]

    @pl.when(i == n_used)
    def _():
        _wait_rows(tme, h_hbm, xbuf.at[slot], sem.at[slot])

        def send(r, c):
            send_row(r, 1 - slot).start()
            return c

        lax.fori_loop(0, tme, send, 0, unroll=ROW_ISSUE_UNROLL)
        wait_sent(1 - slot)
        obuf[slot] = jnp.zeros(obuf.shape[1:], F32)

    @pl.when(i >= n_used)
    def _():
        zs = n_used % 2
        fill = pltpu.make_async_copy(obuf.at[zs], y_hbm.at[pl.ds(pl.multiple_of(i * tme, tme), tme)], osem.at[zs])
        fill.start()
        fill.wait()


def _moe(tile_expert, n_used, buf_tok, dst_rows, h2, w_gu, b_gu, w_d, b_d, tme):
    n_tiles = tile_expert.shape[0]
    D = h2.shape[1]
    tok3 = buf_tok.reshape(n_tiles, 1, tme)
    spare = n_tiles * tme + jnp.arange(tme, dtype=jnp.int32)
    dst3 = jnp.concatenate([spare, dst_rows]).reshape(n_tiles + 1, 1, tme)
    grid_spec = pltpu.PrefetchScalarGridSpec(
        num_scalar_prefetch=2,
        grid=(n_tiles,),
        in_specs=[
            pl.BlockSpec((1, 1, tme), lambda i, te, nu: (i, 0, 0), memory_space=pltpu.SMEM),
            pl.BlockSpec((1, 1, tme), lambda i, te, nu: (jnp.minimum(i + 1, n_tiles - 1), 0, 0),
                         memory_space=pltpu.SMEM),
            pl.BlockSpec((1, 1, tme), lambda i, te, nu: (i, 0, 0), memory_space=pltpu.SMEM),
            pl.BlockSpec(memory_space=pl.ANY),
            pl.BlockSpec(memory_space=pl.ANY),
            pl.BlockSpec((1, 1, 2 * D_EXPERT), lambda i, te, nu: (te[i], 0, 0)),
            pl.BlockSpec(memory_space=pl.ANY),
            pl.BlockSpec((1, 1, D), lambda i, te, nu: (te[i], 0, 0)),
        ],
        out_specs=pl.BlockSpec(memory_space=pl.ANY),
        scratch_shapes=[pltpu.VMEM((2, tme, D), F32), pltpu.SemaphoreType.DMA((2,)),
                        pltpu.VMEM((D, 2 * D_EXPERT), BF16), pltpu.VMEM((D_EXPERT, D), BF16),
                        pltpu.VMEM((2, W_CHUNK, 2 * D_EXPERT), F32), pltpu.VMEM((2, W_CHUNK, D), F32),
                        pltpu.SemaphoreType.DMA((4,)),
                        pltpu.VMEM((2, tme, D), F32), pltpu.SemaphoreType.DMA((2,))],
    )
    return pl.pallas_call(
        functools.partial(_moe_kernel, tme=tme, fc=512),
        grid_spec=grid_spec,
        out_shape=jax.ShapeDtypeStruct(((n_tiles + 1) * tme, D), F32),
        compiler_params=_cparams(("arbitrary",)),
        name="moe",
    )(tile_expert, n_used, tok3, tok3, dst3, h2,
      w_gu, b_gu.reshape(N_EXPERTS, 1, 2 * D_EXPERT), w_d, b_d.reshape(N_EXPERTS, 1, D))


def _combine_kernel(y0_ref, y1_ref, y2_ref, y3_ref, gate_ref, x1_ref, mod_ref, o_ref):
    gates = gate_ref[...]
    y = gates[:, 0:1] * y0_ref[...]
    for kk, y_ref in ((1, y1_ref), (2, y2_ref), (3, y3_ref)):
        y = y + gates[:, kk:kk + 1] * y_ref[...]
    g_f = mod_ref[0][5:6]
    o_ref[...] = x1_ref[...] + g_f * y


def _combine(y_slots, gates_t, x1, mod3, S, tm):
    T, D = x1.shape
    tps = S // tm
    n_tiles = T // tm
    slot_spec = lambda kk: pl.BlockSpec((tm, D), lambda i: (kk * n_tiles + i, 0))
    return pl.pallas_call(
        _combine_kernel,
        grid=(n_tiles,),
        in_specs=[slot_spec(0), slot_spec(1), slot_spec(2), slot_spec(3),
                  pl.BlockSpec((tm, TOP_K), lambda i: (i, 0)),
                  pl.BlockSpec((tm, D), lambda i: (i, 0)),
                  pl.BlockSpec((1, 6, D), lambda i: (i // tps, 0, 0))],
        out_specs=pl.BlockSpec((tm, D), lambda i: (i, 0)),
        out_shape=jax.ShapeDtypeStruct((T, D), F32),
        compiler_params=_cparams(("arbitrary",)),
        name="combine",
    )(y_slots, y_slots, y_slots, y_slots, gates_t, x1, mod3)


def _route(top_i, tme):
    T = top_i.shape[1]
    TK = T * TOP_K
    assert TK % tme == 0
    flat_e = top_i.T.reshape(-1)
    counts = jnp.sum((flat_e[:, None] == jnp.arange(N_EXPERTS, dtype=jnp.int32)[None, :]).astype(jnp.int32), axis=0)
    padded = ((counts + tme - 1) // tme) * tme
    padded_end = jnp.cumsum(padded)
    padded_start = padded_end - padded
    group_end = jnp.cumsum(counts)
    group_start = group_end - counts

    n_tiles = TK // tme + N_EXPERTS
    tile_start = jnp.arange(n_tiles) * tme
    tile_expert = jnp.minimum(jnp.sum(tile_start[:, None] >= padded_end[None, :], axis=1),
                              N_EXPERTS - 1).astype(jnp.int32)
    n_used = (padded_end[-1] // tme).astype(jnp.int32).reshape(1)

    a_sorted = jnp.argsort(flat_e).astype(jnp.int32)
    p = tile_start[:, None] + jnp.arange(tme)[None, :]
    r = p - padded_start[tile_expert][:, None]
    valid = r < counts[tile_expert][:, None]
    src = jnp.clip(group_start[tile_expert][:, None] + r, 0, TK - 1)
    a = a_sorted[src.reshape(-1)].reshape(n_tiles, tme)
    buf_tok = jnp.where(valid, a // TOP_K, 0).astype(jnp.int32).reshape(-1)
    dst_rows = jnp.where(valid, (a % TOP_K) * T + a // TOP_K,
                         TK + p - group_end[tile_expert][:, None]).astype(jnp.int32).reshape(-1)
    return tile_expert, n_used, buf_tok, dst_rows


def _permute_weights(w_in, w_uq, w_ukv, q_head_g, k_head_g):
    half = QK_ROPE // 2
    pe0 = _C_PE
    pe_cols = jnp.concatenate([jnp.arange(pe0, pe0 + half), jnp.arange(pe0, pe0 + half),
                               jnp.arange(pe0 + half, pe0 + 2 * half), jnp.arange(pe0 + half, pe0 + 2 * half)])
    w_in_p = jnp.concatenate([w_in[:, :pe0], w_in[:, pe_cols]], axis=1).astype(BF16)

    hq = jnp.arange(MLA_HEADS)[:, None] * QK_HEAD
    nope_cols = (hq + jnp.arange(QK_NOPE)[None, :]).reshape(-1)
    rope_cols = []
    for j in range(MLA_HEADS // 2):
        for part in range(2):
            for h in (2 * j, 2 * j + 1):
                rope_cols.append(h * QK_HEAD + QK_NOPE + part * half + jnp.arange(half))
    wuq_p = w_uq[:, jnp.concatenate([nope_cols] + rope_cols)].astype(BF16)

    hk = jnp.arange(MLA_HEADS)[:, None] * (QK_NOPE + V_HEAD)
    kn_cols = (hk + jnp.arange(QK_NOPE)[None, :]).reshape(-1)
    v_cols = (hk + QK_NOPE + jnp.arange(V_HEAD)[None, :]).reshape(-1)
    wukv_p = w_ukv[:, jnp.concatenate([kn_cols, v_cols])].astype(BF16)

    def gain_p(g):
        x1, x2 = g[QK_NOPE:QK_NOPE + half], g[QK_NOPE + half:]
        return jnp.concatenate([g[:QK_NOPE], x1, x1, x2, x2]).reshape(1, 2 * LANES)

    return w_in_p, wuq_p, wukv_p, gain_p(q_head_g), gain_p(k_head_g)


def _tile(n, pref):
    t = pref
    while n % t:
        t //= 2
    return t


def kernel(x, c, positions, w_ada, b_ada, norm_mix_g, w_in, conv_w, q_lat_norm_g, w_uq, kv_lat_norm_g, w_ukv, q_head_norm_g, k_head_norm_g, conv_out_norm_g, mla_out_norm_g, w_out, norm_ffn_g, router_w, router_b, w_gate_up, b_gate_up, w_down, b_down):
    B, S, D = x.shape
    T = B * S
    assert D == D_MODEL and w_ada.shape[0] == 1 and S % 128 == 0
    l = 0
    tm = _tile(S, 512)
    ts = 512 if S >= 8192 else S // 8
    tme = 512 if T * TOP_K >= 32768 else 256
    tmc = _tile(S, 256)

    mod3 = _ada(c, w_ada[l], b_ada[l]).reshape(B, 6, D)
    x2 = x.reshape(T, D)

    w_in_p, wuq_p, wukv_p, qg_p, kg_p = _permute_weights(
        w_in[l], w_uq[l], w_ukv[l], q_head_norm_g[l], k_head_norm_g[l])
    half = QK_ROPE // 2
    inv_freq = ROPE_THETA ** (-jnp.arange(half, dtype=F32) / half)
    invf = jnp.tile(inv_freq, 4).reshape(1, LANES)

    mc, qln, kvln, kpe = _inproj(
        x2, mod3, norm_mix_g[l].reshape(1, D), w_in_p, conv_w[l],
        conv_out_norm_g[l].reshape(1, CONV_DIM), q_lat_norm_g[l].reshape(1, Q_LORA),
        kv_lat_norm_g[l].reshape(1, KV_LORA), S, tm)
    q, k, v = _upproj(qln, kvln, kpe, positions.reshape(T, 1), invf, wuq_p, wukv_p, qg_p, kg_p, B, S, tm)
    mm = _attention(q, k, v, mla_out_norm_g[l], B, S, ts, 4, 4 * ts).reshape(T, MLA_DIM)
    x1, h2, top_i, gates = _outproj(
        mc, mm, x2, mod3, w_out[l].astype(BF16), norm_ffn_g[l].reshape(1, D),
        router_w[l].T, router_b[l].reshape(N_EXPERTS, 1), S, tm)

    tile_expert, n_used, buf_tok, dst_rows = _route(top_i, tme)
    y_slots = _moe(tile_expert, n_used, buf_tok, dst_rows, h2, w_gate_up[l], b_gate_up[l], w_down[l], b_down[l], tme)
    out = _combine(y_slots, gates.T, x1, mod3, S, tmc)
    return out.reshape(B, S, D)
```

```python
import functools

import jax
import jax.numpy as jnp
from jax import lax
from jax.experimental import pallas as pl
from jax.experimental.pallas import tpu as pltpu

F32 = jnp.float32
BF16 = jnp.bfloat16

D_MODEL = 2048
CONV_GROUPS = 8
CONV_DIM = 1024
CONV_K = 3
MLA_HEADS = 8
QK_NOPE = 128
QK_ROPE = 64
QK_HEAD = QK_NOPE + QK_ROPE
V_HEAD = 128
MLA_DIM = MLA_HEADS * V_HEAD
Q_LORA = 768
KV_LORA = 512
ROPE_THETA = 10000.0
N_EXPERTS = 32
TOP_K = 4
D_EXPERT = D_MODEL
SWIGLU_LIMIT = 7.0
SWIGLU_ALPHA = 1.702
EPS = 1e-6

LANES = 128
QK_PAD = 2 * LANES
V_EXT = 2 * LANES
LOG2E = 1.4426950408889634
_C_B, _C_C, _C_U = 0, CONV_DIM, 2 * CONV_DIM
_C_Q = 3 * CONV_DIM
_C_KV = _C_Q + Q_LORA
_C_PE = _C_KV + KV_LORA
D_IN_PAD = _C_PE + LANES
VMEM_LIMIT = 56 * 1024 * 1024


def _cparams(sem):
    return pltpu.CompilerParams(dimension_semantics=sem, vmem_limit_bytes=VMEM_LIMIT)


def _resident(shape):
    nd = len(shape)
    return pl.BlockSpec(shape, lambda *_: (0,) * nd, pipeline_mode=pl.Buffered(1))


def _split_bf16(a):
    hi = a.astype(BF16)
    lo = (a - hi.astype(F32)).astype(BF16)
    return hi, lo


def _dot(a, b):
    return jnp.dot(a, b, preferred_element_type=F32)


def _dot_nt(a, b):
    return lax.dot_general(a, b, (((1,), (1,)), ((), ())), preferred_element_type=F32)


def _rms(x, n):
    return lax.rsqrt(jnp.sum(x * x, axis=-1, keepdims=True) * (1.0 / n) + EPS)


def _ada_kernel(c_ref, w_ref, b_ref, o_ref):
    c = c_ref[...]
    cond = c * jax.nn.sigmoid(c)
    c_hi, c_lo = _split_bf16(cond)
    w_hi, w_lo = _split_bf16(w_ref[...])
    o_ref[...] = _dot(c_hi, w_hi) + _dot(c_hi, w_lo) + _dot(c_lo, w_hi) + b_ref[...]


def _ada(c, w_ada, b_ada):
    B, D = c.shape
    n_out = w_ada.shape[1]
    rows = 8
    c_pad = jnp.zeros((rows, D), F32).at[:B].set(c)
    tn = 1024
    out = pl.pallas_call(
        _ada_kernel,
        grid=(n_out // tn,),
        in_specs=[pl.BlockSpec((rows, D), lambda j: (0, 0)),
                  pl.BlockSpec((D, tn), lambda j: (0, j)),
                  pl.BlockSpec((1, tn), lambda j: (0, j))],
        out_specs=pl.BlockSpec((rows, tn), lambda j: (0, j)),
        out_shape=jax.ShapeDtypeStruct((rows, n_out), F32),
        compiler_params=_cparams(("arbitrary",)),
        name="ada",
    )(c_pad, w_ada, b_ada.reshape(1, n_out))
    return out[:B]


def _inproj_kernel(x_ref, mod_ref, g_ref, w_ref, cw_ref, cg_ref, qg_ref, kvg_ref,
                   mc_ref, qln_ref, kvln_ref, kpe_ref, cu_scr, *, tm, tiles_per_seq):
    i = pl.program_id(0)
    x = x_ref[...]
    mod = mod_ref[0]
    sh, sc = mod[0:1], mod[1:2]
    h = (x * _rms(x, D_MODEL) * g_ref[...]) * (1.0 + sc) + sh
    hb = h.astype(BF16)

    def proj(lo, hi):
        return _dot(hb, w_ref[:, lo:hi])

    cu = proj(_C_C, _C_U) * proj(_C_U, _C_Q)

    @pl.when(i % tiles_per_seq == 0)
    def _():
        cu_scr[0:8, :] = jnp.zeros((8, CONV_DIM), F32)

    cu_scr[8:8 + tm, :] = cu
    s1 = cu_scr[7:7 + tm, :]
    s2 = cu_scr[6:6 + tm, :]
    cw = cw_ref[...]
    conv = cw[0:1] * s2 + cw[1:2] * s1 + cw[2:3] * cu
    cu_scr[0:8, :] = cu_scr[tm:tm + 8, :]
    y = proj(_C_B, _C_C) * conv
    cg = cg_ref[...]
    for g in range(CONV_GROUPS):
        sl = slice(g * LANES, (g + 1) * LANES)
        yg = y[:, sl]
        mc_ref[:, sl] = (yg * _rms(yg, LANES) * cg[:, sl]).astype(BF16)

    ql = proj(_C_Q, _C_KV)
    qln_ref[...] = (ql * _rms(ql, Q_LORA) * qg_ref[...]).astype(BF16)
    kvl = proj(_C_KV, _C_PE)
    kvln_ref[...] = (kvl * _rms(kvl, KV_LORA) * kvg_ref[...]).astype(BF16)
    kpe_ref[...] = proj(_C_PE, D_IN_PAD)


def _inproj(x2, mod3, norm_g, w_in_p, conv_w, conv_g, q_lat_g, kv_lat_g, S, tm):
    T, D = x2.shape
    tps = S // tm
    row = lambda n: pl.BlockSpec((tm, n), lambda i: (i, 0))
    return pl.pallas_call(
        functools.partial(_inproj_kernel, tm=tm, tiles_per_seq=tps),
        grid=(T // tm,),
        in_specs=[row(D),
                  pl.BlockSpec((1, 6, D), lambda i: (i // tps, 0, 0)),
                  _resident((1, D)),
                  _resident((D, D_IN_PAD)),
                  _resident((CONV_K, CONV_DIM)),
                  _resident((1, CONV_DIM)),
                  _resident((1, Q_LORA)),
                  _resident((1, KV_LORA))],
        out_specs=[row(CONV_DIM), row(Q_LORA), row(KV_LORA), row(LANES)],
        out_shape=[jax.ShapeDtypeStruct((T, CONV_DIM), BF16),
                   jax.ShapeDtypeStruct((T, Q_LORA), BF16),
                   jax.ShapeDtypeStruct((T, KV_LORA), BF16),
                   jax.ShapeDtypeStruct((T, LANES), F32)],
        scratch_shapes=[pltpu.VMEM((tm + 8, CONV_DIM), F32)],
        compiler_params=_cparams(("arbitrary",)),
        name="inproj",
    )(x2, mod3, norm_g, w_in_p, conv_w, conv_g, q_lat_g, kv_lat_g)


def _upproj_kernel(qln_ref, kvln_ref, kpe_ref, pos_ref, invf_ref, wuq_ref, wukv_ref,
                   qg_ref, kg_ref, q_ref, k_ref, v_ref, *, tm):
    ang = pos_ref[...].astype(F32) * invf_ref[...]
    lane = lax.broadcasted_iota(jnp.int32, (tm, LANES), 1)
    cos = jnp.cos(ang)
    sin_signed = jnp.where(lane < LANES // 2, -1.0, 1.0) * jnp.sin(ang)
    even_head = (lane // (QK_ROPE // 2)) % 2 == 0

    def rope(col):
        return col * cos + pltpu.roll(col, LANES // 2, axis=1) * sin_signed

    qg = qg_ref[...]
    kg = kg_ref[...]
    qscale = QK_HEAD ** -0.5 * LOG2E

    q = _dot(qln_ref[...], wuq_ref[...])
    for j in range(MLA_HEADS // 2):
        rc = q[:, MLA_DIM + j * LANES:MLA_DIM + (j + 1) * LANES]
        rc2 = rc * rc
        ss_even = jnp.sum(jnp.where(even_head, rc2, 0.0), axis=-1, keepdims=True)
        ss_odd = jnp.sum(jnp.where(even_head, 0.0, rc2), axis=-1, keepdims=True)
        r = []
        for h, ss_rope in ((2 * j, ss_even), (2 * j + 1, ss_odd)):
            qn = q[:, h * LANES:(h + 1) * LANES]
            ss = jnp.sum(qn * qn, axis=-1, keepdims=True) + ss_rope
            rh = lax.rsqrt(ss * (1.0 / QK_HEAD) + EPS) * qscale
            r.append(rh)
            q_ref[0, h, :, 0:LANES] = (qn * rh * qg[:, 0:LANES]).astype(BF16)
        roped = rope(rc * jnp.where(even_head, r[0], r[1]) * qg[:, LANES:2 * LANES]).astype(BF16)
        q_ref[0, 2 * j, :, LANES:2 * LANES] = roped
        q_ref[0, 2 * j + 1, :, LANES:2 * LANES] = roped

    kv = _dot(kvln_ref[...], wukv_ref[...])
    kp = kpe_ref[...]
    ss_pe = 0.5 * jnp.sum(kp * kp, axis=-1, keepdims=True)
    kr = rope(kp * kg[:, LANES:2 * LANES])
    for h in range(MLA_HEADS):
        kn = kv[:, h * LANES:(h + 1) * LANES]
        ss = jnp.sum(kn * kn, axis=-1, keepdims=True) + ss_pe
        rh = lax.rsqrt(ss * (1.0 / QK_HEAD) + EPS)
        k_ref[0, h, :, 0:LANES] = (kn * rh * kg[:, 0:LANES]).astype(BF16)
        mine = even_head if h % 2 == 0 else jnp.logical_not(even_head)
        k_ref[0, h, :, LANES:2 * LANES] = jnp.where(mine, kr * rh, 0.0).astype(BF16)
        v_ref[0, h, :, 0:V_HEAD] = kv[:, MLA_DIM + h * LANES:MLA_DIM + (h + 1) * LANES].astype(BF16)
        v_ref[0, h, :, V_HEAD:V_EXT] = jnp.ones((tm, V_EXT - V_HEAD), BF16)


def _upproj(qln, kvln, kpe, pos_col, invf, wuq_p, wukv_p, qg_p, kg_p, B, S, tm):
    T = qln.shape[0]
    tps = S // tm
    row = lambda n: pl.BlockSpec((tm, n), lambda i: (i, 0))
    head = lambda n: pl.BlockSpec((1, MLA_HEADS, tm, n), lambda i: (i // tps, 0, i % tps, 0))
    return pl.pallas_call(
        functools.partial(_upproj_kernel, tm=tm),
        grid=(T // tm,),
        in_specs=[row(Q_LORA), row(KV_LORA), row(LANES), row(1),
                  _resident((1, LANES)),
                  _resident(wuq_p.shape), _resident(wukv_p.shape),
                  _resident((1, 2 * LANES)), _resident((1, 2 * LANES))],
        out_specs=[head(QK_PAD), head(QK_PAD), head(V_EXT)],
        out_shape=[jax.ShapeDtypeStruct((B, MLA_HEADS, S, QK_PAD), BF16),
                   jax.ShapeDtypeStruct((B, MLA_HEADS, S, QK_PAD), BF16),
                   jax.ShapeDtypeStruct((B, MLA_HEADS, S, V_EXT), BF16)],
        compiler_params=_cparams(("arbitrary",)),
        name="upproj",
    )(qln, kvln, kpe, pos_col, invf, wuq_p, wukv_p, qg_p, kg_p)


def _attn_kernel(q_ref, k_ref, v_ref, g_ref, o_ref, m_scr, acc_scr, *, ts, nsub, tk):
    qi = pl.program_id(1)
    tq = ts * nsub
    for a in range(nsub):
        m_scr[a] = jnp.full((ts, LANES), -jnp.inf, F32)
        acc_scr[a] = jnp.zeros((ts, V_EXT), F32)

    def step(a, start, width, masked):
        q = q_ref[0, a * ts:(a + 1) * ts, :]
        k = k_ref[0, pl.ds(start, width), :]
        v = v_ref[0, pl.ds(start, width), :]
        s = _dot_nt(q, k)
        if masked:
            row = lax.broadcasted_iota(jnp.int32, (ts, width), 0) + (qi * tq + a * ts)
            col = lax.broadcasted_iota(jnp.int32, (ts, width), 1) + start
            s = jnp.where(col <= row, s, -jnp.inf)
        m_prev = m_scr[a]
        m_new = jnp.maximum(m_prev, jnp.max(s, axis=-1, keepdims=True))
        p = jnp.exp2(s - jnp.concatenate([m_new] * (width // LANES), axis=1))
        corr = jnp.exp2(m_prev - m_new)
        acc_scr[a] = (jnp.concatenate([corr] * (V_EXT // LANES), axis=1) * acc_scr[a]
                      + _dot(p.astype(BF16), v))
        m_scr[a] = m_new

    def body(j, carry):
        start = pl.multiple_of(j * tk, tk)
        for a in range(nsub):
            step(a, start, tk, False)
        return carry

    lax.fori_loop(0, qi * (tq // tk), body, 0)
    for a in range(nsub):
        for d in range(a + 1):
            step(a, pl.multiple_of(qi * tq + d * ts, ts), ts, d == a)

    for a in range(nsub):
        acc = acc_scr[a]
        o = acc[:, 0:V_HEAD] / acc[:, V_HEAD:V_EXT]
        o_ref[0, a * ts:(a + 1) * ts, :] = (o * _rms(o, V_HEAD) * g_ref[0]).astype(BF16)


def _attention(q, k, v, mla_g, B, S, ts, nsub, tk):
    BH = B * MLA_HEADS
    H = MLA_HEADS
    tq = ts * nsub
    return pl.pallas_call(
        functools.partial(_attn_kernel, ts=ts, nsub=nsub, tk=tk),
        grid=(BH, S // tq),
        in_specs=[pl.BlockSpec((1, tq, QK_PAD), lambda bh, qi: (bh, qi, 0)),
                  pl.BlockSpec((1, S, QK_PAD), lambda bh, qi: (bh, 0, 0)),
                  pl.BlockSpec((1, S, V_EXT), lambda bh, qi: (bh, 0, 0)),
                  pl.BlockSpec((1, 1, V_HEAD), lambda bh, qi: (bh % H, 0, 0))],
        out_specs=pl.BlockSpec((1, tq, V_HEAD), lambda bh, qi: (bh // H, qi, bh % H)),
        out_shape=jax.ShapeDtypeStruct((B, S, MLA_DIM), BF16),
        scratch_shapes=[pltpu.VMEM((nsub, ts, LANES), F32), pltpu.VMEM((nsub, ts, V_EXT), F32)],
        compiler_params=_cparams(("arbitrary", "arbitrary")),
        name="attn",
    )(q.reshape(BH, S, QK_PAD), k.reshape(BH, S, QK_PAD), v.reshape(BH, S, V_EXT),
      mla_g.reshape(H, 1, V_HEAD))


def _outproj_kernel(mc_ref, mm_ref, x_ref, mod_ref, w_ref, g_ref, rw_ref, rb_ref,
                    x1_ref, h2_ref, idx_ref, gate_ref, *, tm):
    mod = mod_ref[0]
    g_m, sh_f, sc_f = mod[2:3], mod[3:4], mod[4:5]
    mix = _dot(mc_ref[...], w_ref[0:CONV_DIM, :]) + _dot(mm_ref[...], w_ref[CONV_DIM:, :])
    x1 = x_ref[...] + g_m * mix
    x1_ref[...] = x1
    h2 = (x1 * _rms(x1, D_MODEL) * g_ref[...]) * (1.0 + sc_f) + sh_f
    h2_ref[...] = h2

    h_hi, h_lo = _split_bf16(h2)
    r_hi, r_lo = _split_bf16(rw_ref[...])
    logits = _dot_nt(r_hi, h_hi) + _dot_nt(r_lo, h_hi) + _dot_nt(r_hi, h_lo) + rb_ref[...]
    eid = lax.broadcasted_iota(jnp.int32, (N_EXPERTS, tm), 0)
    vals = []
    for kk in range(TOP_K):
        mx = jnp.max(logits, axis=0, keepdims=True)
        idx = jnp.min(jnp.where(logits == mx, eid, N_EXPERTS), axis=0, keepdims=True)
        idx_ref[kk:kk + 1, :] = idx
        vals.append(mx)
        logits = jnp.where(eid == idx, -jnp.inf, logits)
    ex = [jnp.exp(vv - vals[0]) for vv in vals]
    denom = ex[0] + ex[1] + ex[2] + ex[3]
    for kk in range(TOP_K):
        gate_ref[kk:kk + 1, :] = ex[kk] / denom


def _outproj(mc, mm, x2, mod3, w_out_b, norm_g, rw_t, rb_col, S, tm):
    T, D = x2.shape
    tps = S // tm
    row = lambda n: pl.BlockSpec((tm, n), lambda i: (i, 0))
    colblk = lambda n: pl.BlockSpec((n, tm), lambda i: (0, i))
    return pl.pallas_call(
        functools.partial(_outproj_kernel, tm=tm),
        grid=(T // tm,),
        in_specs=[row(CONV_DIM), row(MLA_DIM), row(D),
                  pl.BlockSpec((1, 6, D), lambda i: (i // tps, 0, 0)),
                  _resident((D, D)), _resident((1, D)),
                  _resident((N_EXPERTS, D)), _resident((N_EXPERTS, 1))],
        out_specs=[row(D), row(D), colblk(TOP_K), colblk(TOP_K)],
        out_shape=[jax.ShapeDtypeStruct((T, D), F32),
                   jax.ShapeDtypeStruct((T, D), F32),
                   jax.ShapeDtypeStruct((TOP_K, T), jnp.int32),
                   jax.ShapeDtypeStruct((TOP_K, T), F32)],
        compiler_params=_cparams(("arbitrary",)),
        name="outproj",
    )(mc, mm, x2, mod3, w_out_b, norm_g, rw_t, rb_col)


def _row_copy(src_hbm, src_row, dst, sem):
    return pltpu.make_async_copy(src_hbm.at[pl.ds(src_row, 1)], dst, sem)


ROW_ISSUE_UNROLL = 8


def _start_rows(idx_at, n_rows, src_hbm, dst, sem):
    def issue(r, c):
        _row_copy(src_hbm, idx_at(r), dst.at[pl.ds(r, 1)], sem).start()
        return c

    lax.fori_loop(0, n_rows, issue, 0, unroll=ROW_ISSUE_UNROLL)


def _wait_rows(n_rows, src_hbm, dst, sem):
    pltpu.make_async_copy(src_hbm.at[pl.ds(0, n_rows)], dst, sem).wait()


W_CHUNK = 64
W_SLOTS = 4


def _load_expert(e, wgu_hbm, wd_hbm, wgu_s, wd_s, stage_gu, stage_d, wsem):
    chunks = ([(wgu_hbm, wgu_s, stage_gu, 0, c) for c in range(wgu_s.shape[0] // W_CHUNK)]
              + [(wd_hbm, wd_s, stage_d, W_SLOTS, c) for c in range(wd_s.shape[0] // W_CHUNK)])

    def copy(item):
        w_hbm, _, stage, sem0, c = item
        return pltpu.make_async_copy(w_hbm.at[e, pl.ds(c * W_CHUNK, W_CHUNK)], stage.at[c % W_SLOTS],
                                     wsem.at[sem0 + c % W_SLOTS])

    for item in chunks[:W_SLOTS - 1]:
        copy(item).start()
    for n, item in enumerate(chunks):
        if n + W_SLOTS - 1 < len(chunks):
            copy(chunks[n + W_SLOTS - 1]).start()
        copy(item).wait()
        _, w_s, stage, _, c = item
        w_s[c * W_CHUNK:(c + 1) * W_CHUNK, :] = stage[c % W_SLOTS].astype(BF16)


def _moe_kernel(te_ref, nu_ref, tok_ref, tok_next_ref, dst_prev_ref, h_hbm, wgu_hbm, bgu_ref, wd_hbm, bd_ref,
                y_hbm, xbuf, sem, wgu_s, wd_s, stage_gu, stage_d, wsem, obuf, osem, *, tme, fc):
    i = pl.program_id(0)
    n_used = nu_ref[0]
    slot = i % 2

    def send_row(r, src_slot):
        return pltpu.make_async_copy(obuf.at[src_slot, pl.ds(r, 1)],
                                     y_hbm.at[pl.ds(dst_prev_ref[0, 0, r], 1)], osem.at[src_slot])

    def wait_sent(src_slot):
        pltpu.make_async_copy(obuf.at[src_slot], y_hbm.at[pl.ds(0, tme)], osem.at[src_slot]).wait()

    @pl.when(i == 0)
    def _():
        _start_rows(lambda r: tok_ref[0, 0, r], tme, h_hbm, xbuf.at[0], sem.at[0])
        obuf[1] = jnp.zeros(obuf.shape[1:], F32)
        pltpu.make_async_copy(obuf.at[1], y_hbm.at[pl.ds((y_hbm.shape[0] // tme - 1) * tme, tme)], osem.at[0]).start()

    e_cur = te_ref[i]
    changed = jnp.logical_or(i == 0, e_cur != te_ref[jnp.maximum(i - 1, 0)])

    @pl.when(jnp.logical_and(i < n_used, changed))
    def _():
        _load_expert(e_cur, wgu_hbm, wd_hbm, wgu_s, wd_s, stage_gu, stage_d, wsem)

    @pl.when(i < n_used)
    def _():
        _wait_rows(tme, h_hbm, xbuf.at[slot], sem.at[slot])
        x = xbuf[slot].astype(BF16)
        acts = []
        n_chunks = D_EXPERT // fc
        per = 2 * tme // n_chunks
        for c in range(n_chunks):
            for r in range((c % (n_chunks // 2)) * per, (c % (n_chunks // 2) + 1) * per):
                if c < n_chunks // 2:
                    _row_copy(h_hbm, tok_next_ref[0, 0, r], xbuf.at[1 - slot, pl.ds(r, 1)], sem.at[1 - slot]).start()
                else:
                    send_row(r, 1 - slot).start()
            g = _dot(x, wgu_s[:, c * fc:(c + 1) * fc]) + bgu_ref[0, :, c * fc:(c + 1) * fc]
            u = (_dot(x, wgu_s[:, D_EXPERT + c * fc:D_EXPERT + (c + 1) * fc])
                 + bgu_ref[0, :, D_EXPERT + c * fc:D_EXPERT + (c + 1) * fc])
            g = jnp.minimum(g, SWIGLU_LIMIT)
            u = jnp.clip(u, -SWIGLU_LIMIT, SWIGLU_LIMIT)
            acts.append(((u + 1.0) * g * jax.nn.sigmoid(SWIGLU_ALPHA * g)).astype(BF16))
        y = _dot(jnp.concatenate(acts, axis=1), wd_s[...]) + bd_ref[0]
        wait_sent(slot)
        obuf[slot] = y

    @pl.when(i == n_used)
    def _():
        _wait_rows(tme, h_hbm, xbuf.at[slot], sem.at[slot])
        wait_sent(slot)

        def send(r, c):
            send_row(r, 1 - slot).start()
            return c

        lax.fori_loop(0, tme, send, 0, unroll=ROW_ISSUE_UNROLL)
        wait_sent(1 - slot)
        obuf[slot] = jnp.zeros(obuf.shape[1:], F32)

    @pl.when(i >= n_used)
    def _():
        zs = n_used % 2
        fill = pltpu.make_async_copy(obuf.at[zs], y_hbm.at[pl.ds(pl.multiple_of(i * tme, tme), tme)], osem.at[zs])
        fill.start()
        fill.wait()


def _moe(tile_expert, n_used, buf_tok, dst_rows, h2, w_gu, b_gu, w_d, b_d, tme):
    n_tiles = tile_expert.shape[0]
    D = h2.shape[1]
    tok3 = buf_tok.reshape(n_tiles, 1, tme)
    spare = n_tiles * tme + jnp.arange(tme, dtype=jnp.int32)
    dst3 = jnp.concatenate([spare, dst_rows]).reshape(n_tiles + 1, 1, tme)
    grid_spec = pltpu.PrefetchScalarGridSpec(
        num_scalar_prefetch=2,
        grid=(n_tiles,),
        in_specs=[
            pl.BlockSpec((1, 1, tme), lambda i, te, nu: (i, 0, 0), memory_space=pltpu.SMEM),
            pl.BlockSpec((1, 1, tme), lambda i, te, nu: (jnp.minimum(i + 1, n_tiles - 1), 0, 0),
                         memory_space=pltpu.SMEM),
            pl.BlockSpec((1, 1, tme), lambda i, te, nu: (i, 0, 0), memory_space=pltpu.SMEM),
            pl.BlockSpec(memory_space=pl.ANY),
            pl.BlockSpec(memory_space=pl.ANY),
            pl.BlockSpec((1, 1, 2 * D_EXPERT), lambda i, te, nu: (te[i], 0, 0)),
            pl.BlockSpec(memory_space=pl.ANY),
            pl.BlockSpec((1, 1, D), lambda i, te, nu: (te[i], 0, 0)),
        ],
        out_specs=pl.BlockSpec(memory_space=pl.ANY),
        scratch_shapes=[pltpu.VMEM((2, tme, D), F32), pltpu.SemaphoreType.DMA((2,)),
                        pltpu.VMEM((D, 2 * D_EXPERT), BF16), pltpu.VMEM((D_EXPERT, D), BF16),
                        pltpu.VMEM((W_SLOTS, W_CHUNK, 2 * D_EXPERT), F32), pltpu.VMEM((W_SLOTS, W_CHUNK, D), F32),
                        pltpu.SemaphoreType.DMA((2 * W_SLOTS,)),
                        pltpu.VMEM((2, tme, D), F32), pltpu.SemaphoreType.DMA((2,))],
    )
    return pl.pallas_call(
        functools.partial(_moe_kernel, tme=tme, fc=512),
        grid_spec=grid_spec,
        out_shape=jax.ShapeDtypeStruct(((n_tiles + 2) * tme, D), F32),
        compiler_params=_cparams(("arbitrary",)),
        name="moe",
    )(tile_expert, n_used, tok3, tok3, dst3, h2,
      w_gu, b_gu.reshape(N_EXPERTS, 1, 2 * D_EXPERT), w_d, b_d.reshape(N_EXPERTS, 1, D))


def _combine_kernel(y0_ref, y1_ref, y2_ref, y3_ref, gate_ref, x1_ref, mod_ref, o_ref):
    gates = gate_ref[...]
    y = gates[:, 0:1] * y0_ref[...]
    for kk, y_ref in ((1, y1_ref), (2, y2_ref), (3, y3_ref)):
        y = y + gates[:, kk:kk + 1] * y_ref[...]
    g_f = mod_ref[0][5:6]
    o_ref[...] = x1_ref[...] + g_f * y


def _combine(y_slots, gates_t, x1, mod3, S, tm):
    T, D = x1.shape
    tps = S // tm
    n_tiles = T // tm
    slot_spec = lambda kk: pl.BlockSpec((tm, D), lambda i: (kk * n_tiles + i, 0))
    return pl.pallas_call(
        _combine_kernel,
        grid=(n_tiles,),
        in_specs=[slot_spec(0), slot_spec(1), slot_spec(2), slot_spec(3),
                  pl.BlockSpec((tm, TOP_K), lambda i: (i, 0)),
                  pl.BlockSpec((tm, D), lambda i: (i, 0)),
                  pl.BlockSpec((1, 6, D), lambda i: (i // tps, 0, 0))],
        out_specs=pl.BlockSpec((tm, D), lambda i: (i, 0)),
        out_shape=jax.ShapeDtypeStruct((T, D), F32),
        compiler_params=_cparams(("arbitrary",)),
        name="combine",
    )(y_slots, y_slots, y_slots, y_slots, gates_t, x1, mod3)


def _route(top_i, tme):
    T = top_i.shape[1]
    TK = T * TOP_K
    assert TK % tme == 0
    flat_e = top_i.T.reshape(-1)
    counts = jnp.sum((flat_e[:, None] == jnp.arange(N_EXPERTS, dtype=jnp.int32)[None, :]).astype(jnp.int32), axis=0)
    padded = ((counts + tme - 1) // tme) * tme
    padded_end = jnp.cumsum(padded)
    padded_start = padded_end - padded
    group_end = jnp.cumsum(counts)
    group_start = group_end - counts

    n_tiles = TK // tme + N_EXPERTS
    tile_start = jnp.arange(n_tiles) * tme
    tile_expert = jnp.minimum(jnp.sum(tile_start[:, None] >= padded_end[None, :], axis=1),
                              N_EXPERTS - 1).astype(jnp.int32)
    n_used = (padded_end[-1] // tme).astype(jnp.int32).reshape(1)

    a_sorted = jnp.argsort(flat_e).astype(jnp.int32)
    p = tile_start[:, None] + jnp.arange(tme)[None, :]
    r = p - padded_start[tile_expert][:, None]
    valid = r < counts[tile_expert][:, None]
    src = jnp.clip(group_start[tile_expert][:, None] + r, 0, TK - 1)
    a = a_sorted[src.reshape(-1)].reshape(n_tiles, tme)
    buf_tok = jnp.where(valid, a // TOP_K, 0).astype(jnp.int32).reshape(-1)
    dst_rows = jnp.where(valid, (a % TOP_K) * T + a // TOP_K,
                         TK + p - group_end[tile_expert][:, None]).astype(jnp.int32).reshape(-1)
    return tile_expert, n_used, buf_tok, dst_rows


def _permute_weights(w_in, w_uq, w_ukv, q_head_g, k_head_g):
    half = QK_ROPE // 2
    pe0 = _C_PE
    pe_cols = jnp.concatenate([jnp.arange(pe0, pe0 + half), jnp.arange(pe0, pe0 + half),
                               jnp.arange(pe0 + half, pe0 + 2 * half), jnp.arange(pe0 + half, pe0 + 2 * half)])
    w_in_p = jnp.concatenate([w_in[:, :pe0], w_in[:, pe_cols]], axis=1).astype(BF16)

    hq = jnp.arange(MLA_HEADS)[:, None] * QK_HEAD
    nope_cols = (hq + jnp.arange(QK_NOPE)[None, :]).reshape(-1)
    rope_cols = []
    for j in range(MLA_HEADS // 2):
        for part in range(2):
            for h in (2 * j, 2 * j + 1):
                rope_cols.append(h * QK_HEAD + QK_NOPE + part * half + jnp.arange(half))
    wuq_p = w_uq[:, jnp.concatenate([nope_cols] + rope_cols)].astype(BF16)

    hk = jnp.arange(MLA_HEADS)[:, None] * (QK_NOPE + V_HEAD)
    kn_cols = (hk + jnp.arange(QK_NOPE)[None, :]).reshape(-1)
    v_cols = (hk + QK_NOPE + jnp.arange(V_HEAD)[None, :]).reshape(-1)
    wukv_p = w_ukv[:, jnp.concatenate([kn_cols, v_cols])].astype(BF16)

    def gain_p(g):
        x1, x2 = g[QK_NOPE:QK_NOPE + half], g[QK_NOPE + half:]
        return jnp.concatenate([g[:QK_NOPE], x1, x1, x2, x2]).reshape(1, 2 * LANES)

    return w_in_p, wuq_p, wukv_p, gain_p(q_head_g), gain_p(k_head_g)


def _tile(n, pref):
    t = pref
    while n % t:
        t //= 2
    return t


def kernel(x, c, positions, w_ada, b_ada, norm_mix_g, w_in, conv_w, q_lat_norm_g, w_uq, kv_lat_norm_g, w_ukv, q_head_norm_g, k_head_norm_g, conv_out_norm_g, mla_out_norm_g, w_out, norm_ffn_g, router_w, router_b, w_gate_up, b_gate_up, w_down, b_down):
    B, S, D = x.shape
    T = B * S
    assert D == D_MODEL and w_ada.shape[0] == 1 and S % 128 == 0
    l = 0
    tm = _tile(S, 512)
    ts = 512 if S >= 8192 else S // 8
    tme = 512 if T * TOP_K >= 32768 else 256
    tmc = _tile(S, 256)

    mod3 = _ada(c, w_ada[l], b_ada[l]).reshape(B, 6, D)
    x2 = x.reshape(T, D)

    w_in_p, wuq_p, wukv_p, qg_p, kg_p = _permute_weights(
        w_in[l], w_uq[l], w_ukv[l], q_head_norm_g[l], k_head_norm_g[l])
    half = QK_ROPE // 2
    inv_freq = ROPE_THETA ** (-jnp.arange(half, dtype=F32) / half)
    invf = jnp.tile(inv_freq, 4).reshape(1, LANES)

    mc, qln, kvln, kpe = _inproj(
        x2, mod3, norm_mix_g[l].reshape(1, D), w_in_p, conv_w[l],
        conv_out_norm_g[l].reshape(1, CONV_DIM), q_lat_norm_g[l].reshape(1, Q_LORA),
        kv_lat_norm_g[l].reshape(1, KV_LORA), S, tm)
    q, k, v = _upproj(qln, kvln, kpe, positions.reshape(T, 1), invf, wuq_p, wukv_p, qg_p, kg_p, B, S, tm)
    mm = _attention(q, k, v, mla_out_norm_g[l], B, S, ts, 4, 4 * ts).reshape(T, MLA_DIM)
    x1, h2, top_i, gates = _outproj(
        mc, mm, x2, mod3, w_out[l].astype(BF16), norm_ffn_g[l].reshape(1, D),
        router_w[l].T, router_b[l].reshape(N_EXPERTS, 1), S, tm)

    tile_expert, n_used, buf_tok, dst_rows = _route(top_i, tme)
    y_slots = _moe(tile_expert, n_used, buf_tok, dst_rows, h2, w_gate_up[l], b_gate_up[l], w_down[l], b_down[l], tme)
    out = _combine(y_slots, gates.T, x1, mod3, S, tmc)
    return out.reshape(B, S, D)
```

```python
import functools

import jax
import jax.numpy as jnp
from jax import lax
from jax.experimental import pallas as pl
from jax.experimental.pallas import tpu as pltpu

F32 = jnp.float32
BF16 = jnp.bfloat16
U32 = jnp.uint32

D_MODEL = 2048
CONV_GROUPS = 8
CONV_DIM = 1024
CONV_K = 3
MLA_HEADS = 8
QK_NOPE = 128
QK_ROPE = 64
QK_HEAD = QK_NOPE + QK_ROPE
V_HEAD = 128
MLA_DIM = MLA_HEADS * V_HEAD
Q_LORA = 768
KV_LORA = 512
ROPE_THETA = 10000.0
N_EXPERTS = 32
TOP_K = 4
D_EXPERT = D_MODEL
SWIGLU_LIMIT = 7.0
SWIGLU_ALPHA = 1.702
EPS = 1e-6

LANES = 128
QK_PAD = 2 * LANES
V_EXT = 2 * LANES
LOG2E = 1.4426950408889634
_C_B, _C_C, _C_U = 0, CONV_DIM, 2 * CONV_DIM
_C_Q = 3 * CONV_DIM
_C_KV = _C_Q + Q_LORA
_C_PE = _C_KV + KV_LORA
D_IN_PAD = _C_PE + LANES
VMEM_LIMIT = 56 * 1024 * 1024


def _cparams(sem):
    return pltpu.CompilerParams(dimension_semantics=sem, vmem_limit_bytes=VMEM_LIMIT)


def _resident(shape):
    nd = len(shape)
    return pl.BlockSpec(shape, lambda *_: (0,) * nd, pipeline_mode=pl.Buffered(1))


def _split_bf16(a):
    hi = a.astype(BF16)
    lo = (a - hi.astype(F32)).astype(BF16)
    return hi, lo


def _dot(a, b):
    return jnp.dot(a, b, preferred_element_type=F32)


def _dot_nt(a, b):
    return lax.dot_general(a, b, (((1,), (1,)), ((), ())), preferred_element_type=F32)


def _pack_halves(y):
    n = y.shape[1] // 2
    bits = lambda v: lax.bitcast_convert_type(v.astype(BF16).astype(F32), U32)
    return bits(y[:, n:]) | (bits(y[:, :n]) >> 16)


def _unpack_halves(w):
    lo = lax.bitcast_convert_type(w << 16, F32)
    hi = lax.bitcast_convert_type(w & jnp.uint32(0xFFFF0000), F32)
    return lo, hi


def _rms(x, n):
    return lax.rsqrt(jnp.sum(x * x, axis=-1, keepdims=True) * (1.0 / n) + EPS)


def _ada_kernel(c_ref, w_ref, b_ref, o_ref):
    c = c_ref[...]
    cond = c * jax.nn.sigmoid(c)
    c_hi, c_lo = _split_bf16(cond)
    w_hi, w_lo = _split_bf16(w_ref[...])
    o_ref[...] = _dot(c_hi, w_hi) + _dot(c_hi, w_lo) + _dot(c_lo, w_hi) + b_ref[...]


def _ada(c, w_ada, b_ada):
    B, D = c.shape
    n_out = w_ada.shape[1]
    rows = 8
    c_pad = jnp.zeros((rows, D), F32).at[:B].set(c)
    tn = 1024
    out = pl.pallas_call(
        _ada_kernel,
        grid=(n_out // tn,),
        in_specs=[pl.BlockSpec((rows, D), lambda j: (0, 0)),
                  pl.BlockSpec((D, tn), lambda j: (0, j)),
                  pl.BlockSpec((1, tn), lambda j: (0, j))],
        out_specs=pl.BlockSpec((rows, tn), lambda j: (0, j)),
        out_shape=jax.ShapeDtypeStruct((rows, n_out), F32),
        compiler_params=_cparams(("arbitrary",)),
        name="ada",
    )(c_pad, w_ada, b_ada.reshape(1, n_out))
    return out[:B]


def _inproj_kernel(x_ref, mod_ref, g_ref, w_ref, cw_ref, cg_ref, qg_ref, kvg_ref,
                   mc_ref, qln_ref, kvln_ref, kpe_ref, cu_scr, *, tm, tiles_per_seq):
    i = pl.program_id(0)
    x = x_ref[...]
    mod = mod_ref[0]
    sh, sc = mod[0:1], mod[1:2]
    h = (x * _rms(x, D_MODEL) * g_ref[...]) * (1.0 + sc) + sh
    hb = h.astype(BF16)

    def proj(lo, hi):
        return _dot(hb, w_ref[:, lo:hi])

    cu = proj(_C_C, _C_U) * proj(_C_U, _C_Q)

    @pl.when(i % tiles_per_seq == 0)
    def _():
        cu_scr[0:8, :] = jnp.zeros((8, CONV_DIM), F32)

    cu_scr[8:8 + tm, :] = cu
    s1 = cu_scr[7:7 + tm, :]
    s2 = cu_scr[6:6 + tm, :]
    cw = cw_ref[...]
    conv = cw[0:1] * s2 + cw[1:2] * s1 + cw[2:3] * cu
    cu_scr[0:8, :] = cu_scr[tm:tm + 8, :]
    y = proj(_C_B, _C_C) * conv
    cg = cg_ref[...]
    for g in range(CONV_GROUPS):
        sl = slice(g * LANES, (g + 1) * LANES)
        yg = y[:, sl]
        mc_ref[:, sl] = (yg * _rms(yg, LANES) * cg[:, sl]).astype(BF16)

    ql = proj(_C_Q, _C_KV)
    qln_ref[...] = (ql * _rms(ql, Q_LORA) * qg_ref[...]).astype(BF16)
    kvl = proj(_C_KV, _C_PE)
    kvln_ref[...] = (kvl * _rms(kvl, KV_LORA) * kvg_ref[...]).astype(BF16)
    kpe_ref[...] = proj(_C_PE, D_IN_PAD)


def _inproj(x2, mod3, norm_g, w_in_p, conv_w, conv_g, q_lat_g, kv_lat_g, S, tm):
    T, D = x2.shape
    tps = S // tm
    row = lambda n: pl.BlockSpec((tm, n), lambda i: (i, 0))
    return pl.pallas_call(
        functools.partial(_inproj_kernel, tm=tm, tiles_per_seq=tps),
        grid=(T // tm,),
        in_specs=[row(D),
                  pl.BlockSpec((1, 6, D), lambda i: (i // tps, 0, 0)),
                  _resident((1, D)),
                  _resident((D, D_IN_PAD)),
                  _resident((CONV_K, CONV_DIM)),
                  _resident((1, CONV_DIM)),
                  _resident((1, Q_LORA)),
                  _resident((1, KV_LORA))],
        out_specs=[row(CONV_DIM), row(Q_LORA), row(KV_LORA), row(LANES)],
        out_shape=[jax.ShapeDtypeStruct((T, CONV_DIM), BF16),
                   jax.ShapeDtypeStruct((T, Q_LORA), BF16),
                   jax.ShapeDtypeStruct((T, KV_LORA), BF16),
                   jax.ShapeDtypeStruct((T, LANES), F32)],
        scratch_shapes=[pltpu.VMEM((tm + 8, CONV_DIM), F32)],
        compiler_params=_cparams(("arbitrary",)),
        name="inproj",
    )(x2, mod3, norm_g, w_in_p, conv_w, conv_g, q_lat_g, kv_lat_g)


def _upproj_kernel(qln_ref, kvln_ref, kpe_ref, pos_ref, invf_ref, wuq_ref, wukv_ref,
                   qg_ref, kg_ref, q_ref, k_ref, v_ref, *, tm):
    ang = pos_ref[...].astype(F32) * invf_ref[...]
    lane = lax.broadcasted_iota(jnp.int32, (tm, LANES), 1)
    cos = jnp.cos(ang)
    sin_signed = jnp.where(lane < LANES // 2, -1.0, 1.0) * jnp.sin(ang)
    even_head = (lane // (QK_ROPE // 2)) % 2 == 0

    def rope(col):
        return col * cos + pltpu.roll(col, LANES // 2, axis=1) * sin_signed

    qg = qg_ref[...]
    kg = kg_ref[...]
    qscale = QK_HEAD ** -0.5 * LOG2E

    q = _dot(qln_ref[...], wuq_ref[...])
    for j in range(MLA_HEADS // 2):
        rc = q[:, MLA_DIM + j * LANES:MLA_DIM + (j + 1) * LANES]
        rc2 = rc * rc
        ss_even = jnp.sum(jnp.where(even_head, rc2, 0.0), axis=-1, keepdims=True)
        ss_odd = jnp.sum(jnp.where(even_head, 0.0, rc2), axis=-1, keepdims=True)
        r = []
        for h, ss_rope in ((2 * j, ss_even), (2 * j + 1, ss_odd)):
            qn = q[:, h * LANES:(h + 1) * LANES]
            ss = jnp.sum(qn * qn, axis=-1, keepdims=True) + ss_rope
            rh = lax.rsqrt(ss * (1.0 / QK_HEAD) + EPS) * qscale
            r.append(rh)
            q_ref[0, h, :, 0:LANES] = (qn * rh * qg[:, 0:LANES]).astype(BF16)
        roped = rope(rc * jnp.where(even_head, r[0], r[1]) * qg[:, LANES:2 * LANES]).astype(BF16)
        q_ref[0, 2 * j, :, LANES:2 * LANES] = roped
        q_ref[0, 2 * j + 1, :, LANES:2 * LANES] = roped

    kv = _dot(kvln_ref[...], wukv_ref[...])
    kp = kpe_ref[...]
    ss_pe = 0.5 * jnp.sum(kp * kp, axis=-1, keepdims=True)
    kr = rope(kp * kg[:, LANES:2 * LANES])
    for h in range(MLA_HEADS):
        kn = kv[:, h * LANES:(h + 1) * LANES]
        ss = jnp.sum(kn * kn, axis=-1, keepdims=True) + ss_pe
        rh = lax.rsqrt(ss * (1.0 / QK_HEAD) + EPS)
        k_ref[0, h, :, 0:LANES] = (kn * rh * kg[:, 0:LANES]).astype(BF16)
        mine = even_head if h % 2 == 0 else jnp.logical_not(even_head)
        k_ref[0, h, :, LANES:2 * LANES] = jnp.where(mine, kr * rh, 0.0).astype(BF16)
        v_ref[0, h, :, 0:V_HEAD] = kv[:, MLA_DIM + h * LANES:MLA_DIM + (h + 1) * LANES].astype(BF16)
        v_ref[0, h, :, V_HEAD:V_EXT] = jnp.ones((tm, V_EXT - V_HEAD), BF16)


def _upproj(qln, kvln, kpe, pos_col, invf, wuq_p, wukv_p, qg_p, kg_p, B, S, tm):
    T = qln.shape[0]
    tps = S // tm
    row = lambda n: pl.BlockSpec((tm, n), lambda i: (i, 0))
    head = lambda n: pl.BlockSpec((1, MLA_HEADS, tm, n), lambda i: (i // tps, 0, i % tps, 0))
    return pl.pallas_call(
        functools.partial(_upproj_kernel, tm=tm),
        grid=(T // tm,),
        in_specs=[row(Q_LORA), row(KV_LORA), row(LANES), row(1),
                  _resident((1, LANES)),
                  _resident(wuq_p.shape), _resident(wukv_p.shape),
                  _resident((1, 2 * LANES)), _resident((1, 2 * LANES))],
        out_specs=[head(QK_PAD), head(QK_PAD), head(V_EXT)],
        out_shape=[jax.ShapeDtypeStruct((B, MLA_HEADS, S, QK_PAD), BF16),
                   jax.ShapeDtypeStruct((B, MLA_HEADS, S, QK_PAD), BF16),
                   jax.ShapeDtypeStruct((B, MLA_HEADS, S, V_EXT), BF16)],
        compiler_params=_cparams(("arbitrary",)),
        name="upproj",
    )(qln, kvln, kpe, pos_col, invf, wuq_p, wukv_p, qg_p, kg_p)


def _attn_kernel(q_ref, k_ref, v_ref, g_ref, o_ref, m_scr, acc_scr, *, ts, nsub, tk):
    qi = pl.program_id(1)
    tq = ts * nsub
    for a in range(nsub):
        m_scr[a] = jnp.full((ts, LANES), -jnp.inf, F32)
        acc_scr[a] = jnp.zeros((ts, V_EXT), F32)

    def step(a, start, width, masked):
        q = q_ref[0, a * ts:(a + 1) * ts, :]
        k = k_ref[0, pl.ds(start, width), :]
        v = v_ref[0, pl.ds(start, width), :]
        s = _dot_nt(q, k)
        if masked:
            row = lax.broadcasted_iota(jnp.int32, (ts, width), 0) + (qi * tq + a * ts)
            col = lax.broadcasted_iota(jnp.int32, (ts, width), 1) + start
            s = jnp.where(col <= row, s, -jnp.inf)
        m_prev = m_scr[a]
        m_new = jnp.maximum(m_prev, jnp.max(s, axis=-1, keepdims=True))
        p = jnp.exp2(s - jnp.concatenate([m_new] * (width // LANES), axis=1))
        corr = jnp.exp2(m_prev - m_new)
        acc_scr[a] = (jnp.concatenate([corr] * (V_EXT // LANES), axis=1) * acc_scr[a]
                      + _dot(p.astype(BF16), v))
        m_scr[a] = m_new

    def body(j, carry):
        start = pl.multiple_of(j * tk, tk)
        for a in range(nsub):
            step(a, start, tk, False)
        return carry

    lax.fori_loop(0, qi * (tq // tk), body, 0)
    for a in range(nsub):
        for d in range(a + 1):
            step(a, pl.multiple_of(qi * tq + d * ts, ts), ts, d == a)

    for a in range(nsub):
        acc = acc_scr[a]
        o = acc[:, 0:V_HEAD] / acc[:, V_HEAD:V_EXT]
        o_ref[0, a * ts:(a + 1) * ts, :] = (o * _rms(o, V_HEAD) * g_ref[0]).astype(BF16)


def _attention(q, k, v, mla_g, B, S, ts, nsub, tk):
    BH = B * MLA_HEADS
    H = MLA_HEADS
    tq = ts * nsub
    return pl.pallas_call(
        functools.partial(_attn_kernel, ts=ts, nsub=nsub, tk=tk),
        grid=(BH, S // tq),
        in_specs=[pl.BlockSpec((1, tq, QK_PAD), lambda bh, qi: (bh, qi, 0)),
                  pl.BlockSpec((1, S, QK_PAD), lambda bh, qi: (bh, 0, 0)),
                  pl.BlockSpec((1, S, V_EXT), lambda bh, qi: (bh, 0, 0)),
                  pl.BlockSpec((1, 1, V_HEAD), lambda bh, qi: (bh % H, 0, 0))],
        out_specs=pl.BlockSpec((1, tq, V_HEAD), lambda bh, qi: (bh // H, qi, bh % H)),
        out_shape=jax.ShapeDtypeStruct((B, S, MLA_DIM), BF16),
        scratch_shapes=[pltpu.VMEM((nsub, ts, LANES), F32), pltpu.VMEM((nsub, ts, V_EXT), F32)],
        compiler_params=_cparams(("arbitrary", "arbitrary")),
        name="attn",
    )(q.reshape(BH, S, QK_PAD), k.reshape(BH, S, QK_PAD), v.reshape(BH, S, V_EXT),
      mla_g.reshape(H, 1, V_HEAD))


def _outproj_kernel(mc_ref, mm_ref, x_ref, mod_ref, w_ref, g_ref, rw_ref, rb_ref,
                    x1_ref, h2_ref, idx_ref, gate_ref, *, tm):
    mod = mod_ref[0]
    g_m, sh_f, sc_f = mod[2:3], mod[3:4], mod[4:5]
    mix = _dot(mc_ref[...], w_ref[0:CONV_DIM, :]) + _dot(mm_ref[...], w_ref[CONV_DIM:, :])
    x1 = x_ref[...] + g_m * mix
    x1_ref[...] = x1
    h2 = (x1 * _rms(x1, D_MODEL) * g_ref[...]) * (1.0 + sc_f) + sh_f
    h2_ref[...] = h2

    h_hi, h_lo = _split_bf16(h2)
    r_hi, r_lo = _split_bf16(rw_ref[...])
    logits = _dot_nt(r_hi, h_hi) + _dot_nt(r_lo, h_hi) + _dot_nt(r_hi, h_lo) + rb_ref[...]
    eid = lax.broadcasted_iota(jnp.int32, (N_EXPERTS, tm), 0)
    vals = []
    for kk in range(TOP_K):
        mx = jnp.max(logits, axis=0, keepdims=True)
        idx = jnp.min(jnp.where(logits == mx, eid, N_EXPERTS), axis=0, keepdims=True)
        idx_ref[kk:kk + 1, :] = idx
        vals.append(mx)
        logits = jnp.where(eid == idx, -jnp.inf, logits)
    ex = [jnp.exp(vv - vals[0]) for vv in vals]
    denom = ex[0] + ex[1] + ex[2] + ex[3]
    for kk in range(TOP_K):
        gate_ref[kk:kk + 1, :] = ex[kk] / denom


def _outproj(mc, mm, x2, mod3, w_out_b, norm_g, rw_t, rb_col, S, tm):
    T, D = x2.shape
    tps = S // tm
    row = lambda n: pl.BlockSpec((tm, n), lambda i: (i, 0))
    colblk = lambda n: pl.BlockSpec((n, tm), lambda i: (0, i))
    return pl.pallas_call(
        functools.partial(_outproj_kernel, tm=tm),
        grid=(T // tm,),
        in_specs=[row(CONV_DIM), row(MLA_DIM), row(D),
                  pl.BlockSpec((1, 6, D), lambda i: (i // tps, 0, 0)),
                  _resident((D, D)), _resident((1, D)),
                  _resident((N_EXPERTS, D)), _resident((N_EXPERTS, 1))],
        out_specs=[row(D), row(D), colblk(TOP_K), colblk(TOP_K)],
        out_shape=[jax.ShapeDtypeStruct((T, D), F32),
                   jax.ShapeDtypeStruct((T, D), F32),
                   jax.ShapeDtypeStruct((TOP_K, T), jnp.int32),
                   jax.ShapeDtypeStruct((TOP_K, T), F32)],
        compiler_params=_cparams(("arbitrary",)),
        name="outproj",
    )(mc, mm, x2, mod3, w_out_b, norm_g, rw_t, rb_col)


def _row_copy(src_hbm, src_row, dst, sem):
    return pltpu.make_async_copy(src_hbm.at[pl.ds(src_row, 1)], dst, sem)


ROW_ISSUE_UNROLL = 8


def _start_rows(idx_at, n_rows, src_hbm, dst, sem):
    def issue(r, c):
        _row_copy(src_hbm, idx_at(r), dst.at[pl.ds(r, 1)], sem).start()
        return c

    lax.fori_loop(0, n_rows, issue, 0, unroll=ROW_ISSUE_UNROLL)


def _wait_rows(n_rows, src_hbm, dst, sem):
    pltpu.make_async_copy(src_hbm.at[pl.ds(0, n_rows)], dst, sem).wait()


W_CHUNK = 64
W_SLOTS = 4


def _load_expert(e, wgu_hbm, wd_hbm, wgu_s, wd_s, stage_gu, stage_d, wsem):
    chunks = ([(wgu_hbm, wgu_s, stage_gu, 0, c) for c in range(wgu_s.shape[0] // W_CHUNK)]
              + [(wd_hbm, wd_s, stage_d, W_SLOTS, c) for c in range(wd_s.shape[0] // W_CHUNK)])

    def copy(item):
        w_hbm, _, stage, sem0, c = item
        return pltpu.make_async_copy(w_hbm.at[e, pl.ds(c * W_CHUNK, W_CHUNK)], stage.at[c % W_SLOTS],
                                     wsem.at[sem0 + c % W_SLOTS])

    for item in chunks[:W_SLOTS - 1]:
        copy(item).start()
    for n, item in enumerate(chunks):
        if n + W_SLOTS - 1 < len(chunks):
            copy(chunks[n + W_SLOTS - 1]).start()
        copy(item).wait()
        _, w_s, stage, _, c = item
        w_s[c * W_CHUNK:(c + 1) * W_CHUNK, :] = stage[c % W_SLOTS].astype(BF16)


def _moe_kernel(te_ref, nu_ref, tok_ref, tok_next_ref, dst_prev_ref, h_hbm, wgu_hbm, bgu_ref, wd_hbm, bd_ref,
                y_hbm, xbuf, sem, wgu_s, wd_s, stage_gu, stage_d, wsem, obuf, osem, *, tme, fc):
    i = pl.program_id(0)
    n_used = nu_ref[0]
    slot = i % 2

    def send_row(r, src_slot):
        return pltpu.make_async_copy(obuf.at[src_slot, pl.ds(r, 1)],
                                     y_hbm.at[pl.ds(dst_prev_ref[0, 0, r], 1)], osem.at[src_slot])

    def wait_sent(src_slot):
        pltpu.make_async_copy(obuf.at[src_slot], y_hbm.at[pl.ds(0, tme)], osem.at[src_slot]).wait()

    @pl.when(i == 0)
    def _():
        _start_rows(lambda r: tok_ref[0, 0, r], tme, h_hbm, xbuf.at[0], sem.at[0])
        obuf[1] = jnp.zeros(obuf.shape[1:], U32)
        pltpu.make_async_copy(obuf.at[1], y_hbm.at[pl.ds((y_hbm.shape[0] // tme - 1) * tme, tme)], osem.at[0]).start()

    e_cur = te_ref[i]
    changed = jnp.logical_or(i == 0, e_cur != te_ref[jnp.maximum(i - 1, 0)])

    @pl.when(jnp.logical_and(i < n_used, changed))
    def _():
        _load_expert(e_cur, wgu_hbm, wd_hbm, wgu_s, wd_s, stage_gu, stage_d, wsem)

    @pl.when(i < n_used)
    def _():
        _wait_rows(tme, h_hbm, xbuf.at[slot], sem.at[slot])
        x = xbuf[slot].astype(BF16)
        acts = []
        n_chunks = D_EXPERT // fc
        per = 2 * tme // n_chunks
        for c in range(n_chunks):
            for r in range((c % (n_chunks // 2)) * per, (c % (n_chunks // 2) + 1) * per):
                if c < n_chunks // 2:
                    _row_copy(h_hbm, tok_next_ref[0, 0, r], xbuf.at[1 - slot, pl.ds(r, 1)], sem.at[1 - slot]).start()
                else:
                    send_row(r, 1 - slot).start()
            g = _dot(x, wgu_s[:, c * fc:(c + 1) * fc]) + bgu_ref[0, :, c * fc:(c + 1) * fc]
            u = (_dot(x, wgu_s[:, D_EXPERT + c * fc:D_EXPERT + (c + 1) * fc])
                 + bgu_ref[0, :, D_EXPERT + c * fc:D_EXPERT + (c + 1) * fc])
            g = jnp.minimum(g, SWIGLU_LIMIT)
            u = jnp.clip(u, -SWIGLU_LIMIT, SWIGLU_LIMIT)
            acts.append(((u + 1.0) * g * jax.nn.sigmoid(SWIGLU_ALPHA * g)).astype(BF16))
        y = _dot(jnp.concatenate(acts, axis=1), wd_s[...]) + bd_ref[0]
        wait_sent(slot)
        obuf[slot] = _pack_halves(y)

    @pl.when(i == n_used)
    def _():
        _wait_rows(tme, h_hbm, xbuf.at[slot], sem.at[slot])
        wait_sent(slot)

        def send(r, c):
            send_row(r, 1 - slot).start()
            return c

        lax.fori_loop(0, tme, send, 0, unroll=ROW_ISSUE_UNROLL)
        wait_sent(1 - slot)
        obuf[slot] = jnp.zeros(obuf.shape[1:], U32)

    @pl.when(i >= n_used)
    def _():
        zs = n_used % 2
        fill = pltpu.make_async_copy(obuf.at[zs], y_hbm.at[pl.ds(pl.multiple_of(i * tme, tme), tme)], osem.at[zs])
        fill.start()
        fill.wait()


def _moe(tile_expert, n_used, buf_tok, dst_rows, h2, w_gu, b_gu, w_d, b_d, tme):
    n_tiles = tile_expert.shape[0]
    D = h2.shape[1]
    tok3 = buf_tok.reshape(n_tiles, 1, tme)
    spare = n_tiles * tme + jnp.arange(tme, dtype=jnp.int32)
    dst3 = jnp.concatenate([spare, dst_rows]).reshape(n_tiles + 1, 1, tme)
    grid_spec = pltpu.PrefetchScalarGridSpec(
        num_scalar_prefetch=2,
        grid=(n_tiles,),
        in_specs=[
            pl.BlockSpec((1, 1, tme), lambda i, te, nu: (i, 0, 0), memory_space=pltpu.SMEM),
            pl.BlockSpec((1, 1, tme), lambda i, te, nu: (jnp.minimum(i + 1, n_tiles - 1), 0, 0),
                         memory_space=pltpu.SMEM),
            pl.BlockSpec((1, 1, tme), lambda i, te, nu: (i, 0, 0), memory_space=pltpu.SMEM),
            pl.BlockSpec(memory_space=pl.ANY),
            pl.BlockSpec(memory_space=pl.ANY),
            pl.BlockSpec((1, 1, 2 * D_EXPERT), lambda i, te, nu: (te[i], 0, 0)),
            pl.BlockSpec(memory_space=pl.ANY),
            pl.BlockSpec((1, 1, D), lambda i, te, nu: (te[i], 0, 0)),
        ],
        out_specs=pl.BlockSpec(memory_space=pl.ANY),
        scratch_shapes=[pltpu.VMEM((2, tme, D), F32), pltpu.SemaphoreType.DMA((2,)),
                        pltpu.VMEM((D, 2 * D_EXPERT), BF16), pltpu.VMEM((D_EXPERT, D), BF16),
                        pltpu.VMEM((W_SLOTS, W_CHUNK, 2 * D_EXPERT), F32), pltpu.VMEM((W_SLOTS, W_CHUNK, D), F32),
                        pltpu.SemaphoreType.DMA((2 * W_SLOTS,)),
                        pltpu.VMEM((2, tme, D // 2), U32), pltpu.SemaphoreType.DMA((2,))],
    )
    return pl.pallas_call(
        functools.partial(_moe_kernel, tme=tme, fc=512),
        grid_spec=grid_spec,
        out_shape=jax.ShapeDtypeStruct(((n_tiles + 2) * tme, D // 2), U32),
        compiler_params=_cparams(("arbitrary",)),
        name="moe",
    )(tile_expert, n_used, tok3, tok3, dst3, h2,
      w_gu, b_gu.reshape(N_EXPERTS, 1, 2 * D_EXPERT), w_d, b_d.reshape(N_EXPERTS, 1, D))


def _combine_kernel(y0_ref, y1_ref, y2_ref, y3_ref, gate_ref, x1_ref, mod_ref, o_ref):
    gates = gate_ref[...]
    n = o_ref.shape[1] // 2
    y_lo = y_hi = None
    for kk, y_ref in enumerate((y0_ref, y1_ref, y2_ref, y3_ref)):
        lo, hi = _unpack_halves(y_ref[...])
        g = gates[:, kk:kk + 1]
        y_lo = g * lo if y_lo is None else y_lo + g * lo
        y_hi = g * hi if y_hi is None else y_hi + g * hi
    g_f = mod_ref[0][5:6]
    o_ref[:, 0:n] = x1_ref[:, 0:n] + g_f[:, 0:n] * y_lo
    o_ref[:, n:] = x1_ref[:, n:] + g_f[:, n:] * y_hi


def _combine(y_slots, gates_t, x1, mod3, S, tm):
    T, D = x1.shape
    tps = S // tm
    n_tiles = T // tm
    slot_spec = lambda kk: pl.BlockSpec((tm, D // 2), lambda i: (kk * n_tiles + i, 0))
    return pl.pallas_call(
        _combine_kernel,
        grid=(n_tiles,),
        in_specs=[slot_spec(0), slot_spec(1), slot_spec(2), slot_spec(3),
                  pl.BlockSpec((tm, TOP_K), lambda i: (i, 0)),
                  pl.BlockSpec((tm, D), lambda i: (i, 0)),
                  pl.BlockSpec((1, 6, D), lambda i: (i // tps, 0, 0))],
        out_specs=pl.BlockSpec((tm, D), lambda i: (i, 0)),
        out_shape=jax.ShapeDtypeStruct((T, D), F32),
        compiler_params=_cparams(("arbitrary",)),
        name="combine",
    )(y_slots, y_slots, y_slots, y_slots, gates_t, x1, mod3)


def _route(top_i, tme):
    T = top_i.shape[1]
    TK = T * TOP_K
    assert TK % tme == 0
    flat_e = top_i.T.reshape(-1)
    counts = jnp.sum((flat_e[:, None] == jnp.arange(N_EXPERTS, dtype=jnp.int32)[None, :]).astype(jnp.int32), axis=0)
    padded = ((counts + tme - 1) // tme) * tme
    padded_end = jnp.cumsum(padded)
    padded_start = padded_end - padded
    group_end = jnp.cumsum(counts)
    group_start = group_end - counts

    n_tiles = TK // tme + N_EXPERTS
    tile_start = jnp.arange(n_tiles) * tme
    tile_expert = jnp.minimum(jnp.sum(tile_start[:, None] >= padded_end[None, :], axis=1),
                              N_EXPERTS - 1).astype(jnp.int32)
    n_used = (padded_end[-1] // tme).astype(jnp.int32).reshape(1)

    a_sorted = jnp.argsort(flat_e).astype(jnp.int32)
    p = tile_start[:, None] + jnp.arange(tme)[None, :]
    r = p - padded_start[tile_expert][:, None]
    valid = r < counts[tile_expert][:, None]
    src = jnp.clip(group_start[tile_expert][:, None] + r, 0, TK - 1)
    a = a_sorted[src.reshape(-1)].reshape(n_tiles, tme)
    buf_tok = jnp.where(valid, a // TOP_K, 0).astype(jnp.int32).reshape(-1)
    dst_rows = jnp.where(valid, (a % TOP_K) * T + a // TOP_K,
                         TK + p - group_end[tile_expert][:, None]).astype(jnp.int32).reshape(-1)
    return tile_expert, n_used, buf_tok, dst_rows


def _permute_weights(w_in, w_uq, w_ukv, q_head_g, k_head_g):
    half = QK_ROPE // 2
    pe0 = _C_PE
    pe_cols = jnp.concatenate([jnp.arange(pe0, pe0 + half), jnp.arange(pe0, pe0 + half),
                               jnp.arange(pe0 + half, pe0 + 2 * half), jnp.arange(pe0 + half, pe0 + 2 * half)])
    w_in_p = jnp.concatenate([w_in[:, :pe0], w_in[:, pe_cols]], axis=1).astype(BF16)

    hq = jnp.arange(MLA_HEADS)[:, None] * QK_HEAD
    nope_cols = (hq + jnp.arange(QK_NOPE)[None, :]).reshape(-1)
    rope_cols = []
    for j in range(MLA_HEADS // 2):
        for part in range(2):
            for h in (2 * j, 2 * j + 1):
                rope_cols.append(h * QK_HEAD + QK_NOPE + part * half + jnp.arange(half))
    wuq_p = w_uq[:, jnp.concatenate([nope_cols] + rope_cols)].astype(BF16)

    hk = jnp.arange(MLA_HEADS)[:, None] * (QK_NOPE + V_HEAD)
    kn_cols = (hk + jnp.arange(QK_NOPE)[None, :]).reshape(-1)
    v_cols = (hk + QK_NOPE + jnp.arange(V_HEAD)[None, :]).reshape(-1)
    wukv_p = w_ukv[:, jnp.concatenate([kn_cols, v_cols])].astype(BF16)

    def gain_p(g):
        x1, x2 = g[QK_NOPE:QK_NOPE + half], g[QK_NOPE + half:]
        return jnp.concatenate([g[:QK_NOPE], x1, x1, x2, x2]).reshape(1, 2 * LANES)

    return w_in_p, wuq_p, wukv_p, gain_p(q_head_g), gain_p(k_head_g)


def _tile(n, pref):
    t = pref
    while n % t:
        t //= 2
    return t


def kernel(x, c, positions, w_ada, b_ada, norm_mix_g, w_in, conv_w, q_lat_norm_g, w_uq, kv_lat_norm_g, w_ukv, q_head_norm_g, k_head_norm_g, conv_out_norm_g, mla_out_norm_g, w_out, norm_ffn_g, router_w, router_b, w_gate_up, b_gate_up, w_down, b_down):
    B, S, D = x.shape
    T = B * S
    assert D == D_MODEL and w_ada.shape[0] == 1 and S % 128 == 0
    l = 0
    tm = _tile(S, 512)
    ts = 512 if S >= 8192 else S // 8
    tme = 512 if T * TOP_K >= 32768 else 256
    tmc = _tile(S, 256)

    mod3 = _ada(c, w_ada[l], b_ada[l]).reshape(B, 6, D)
    x2 = x.reshape(T, D)

    w_in_p, wuq_p, wukv_p, qg_p, kg_p = _permute_weights(
        w_in[l], w_uq[l], w_ukv[l], q_head_norm_g[l], k_head_norm_g[l])
    half = QK_ROPE // 2
    inv_freq = ROPE_THETA ** (-jnp.arange(half, dtype=F32) / half)
    invf = jnp.tile(inv_freq, 4).reshape(1, LANES)

    mc, qln, kvln, kpe = _inproj(
        x2, mod3, norm_mix_g[l].reshape(1, D), w_in_p, conv_w[l],
        conv_out_norm_g[l].reshape(1, CONV_DIM), q_lat_norm_g[l].reshape(1, Q_LORA),
        kv_lat_norm_g[l].reshape(1, KV_LORA), S, tm)
    q, k, v = _upproj(qln, kvln, kpe, positions.reshape(T, 1), invf, wuq_p, wukv_p, qg_p, kg_p, B, S, tm)
    mm = _attention(q, k, v, mla_out_norm_g[l], B, S, ts, 4, 4 * ts).reshape(T, MLA_DIM)
    x1, h2, top_i, gates = _outproj(
        mc, mm, x2, mod3, w_out[l].astype(BF16), norm_ffn_g[l].reshape(1, D),
        router_w[l].T, router_b[l].reshape(N_EXPERTS, 1), S, tm)

    tile_expert, n_used, buf_tok, dst_rows = _route(top_i, tme)
    y_slots = _moe(tile_expert, n_used, buf_tok, dst_rows, h2, w_gate_up[l], b_gate_up[l], w_down[l], b_down[l], tme)
    out = _combine(y_slots, gates.T, x1, mod3, S, tmc)
    return out.reshape(B, S, D)
```

```python
import functools

import jax
import jax.numpy as jnp
from jax import lax
from jax.experimental import pallas as pl
from jax.experimental.pallas import tpu as pltpu

F32 = jnp.float32
BF16 = jnp.bfloat16
U32 = jnp.uint32

D_MODEL = 2048
CONV_GROUPS = 8
CONV_DIM = 1024
CONV_K = 3
MLA_HEADS = 8
QK_NOPE = 128
QK_ROPE = 64
QK_HEAD = QK_NOPE + QK_ROPE
V_HEAD = 128
MLA_DIM = MLA_HEADS * V_HEAD
Q_LORA = 768
KV_LORA = 512
ROPE_THETA = 10000.0
N_EXPERTS = 32
TOP_K = 4
D_EXPERT = D_MODEL
SWIGLU_LIMIT = 7.0
SWIGLU_ALPHA = 1.702
EPS = 1e-6

LANES = 128
QK_PAD = 2 * LANES
V_EXT = 2 * LANES
LOG2E = 1.4426950408889634
_C_B, _C_C, _C_U = 0, CONV_DIM, 2 * CONV_DIM
_C_Q = 3 * CONV_DIM
_C_KV = _C_Q + Q_LORA
_C_PE = _C_KV + KV_LORA
D_IN_PAD = _C_PE + LANES
VMEM_LIMIT = 56 * 1024 * 1024


def _cparams(sem):
    return pltpu.CompilerParams(dimension_semantics=sem, vmem_limit_bytes=VMEM_LIMIT)


def _resident(shape):
    nd = len(shape)
    return pl.BlockSpec(shape, lambda *_: (0,) * nd, pipeline_mode=pl.Buffered(1))


def _split_bf16(a):
    hi = a.astype(BF16)
    lo = (a - hi.astype(F32)).astype(BF16)
    return hi, lo


def _dot(a, b):
    return jnp.dot(a, b, preferred_element_type=F32)


def _dot_nt(a, b):
    return lax.dot_general(a, b, (((1,), (1,)), ((), ())), preferred_element_type=F32)


def _pack_halves(y):
    n = y.shape[1] // 2
    bits = lambda v: lax.bitcast_convert_type(v.astype(BF16).astype(F32), U32)
    return bits(y[:, n:]) | (bits(y[:, :n]) >> 16)


def _unpack_halves(w):
    lo = lax.bitcast_convert_type(w << 16, F32)
    hi = lax.bitcast_convert_type(w & jnp.uint32(0xFFFF0000), F32)
    return lo, hi


def _rms(x, n):
    return lax.rsqrt(jnp.sum(x * x, axis=-1, keepdims=True) * (1.0 / n) + EPS)


def _ada_kernel(c_ref, w_ref, b_ref, o_ref):
    c = c_ref[...]
    cond = c * jax.nn.sigmoid(c)
    c_hi, c_lo = _split_bf16(cond)
    w_hi, w_lo = _split_bf16(w_ref[...])
    o_ref[...] = _dot(c_hi, w_hi) + _dot(c_hi, w_lo) + _dot(c_lo, w_hi) + b_ref[...]


def _ada(c, w_ada, b_ada):
    B, D = c.shape
    n_out = w_ada.shape[1]
    rows = 8
    c_pad = jnp.zeros((rows, D), F32).at[:B].set(c)
    tn = 1024
    out = pl.pallas_call(
        _ada_kernel,
        grid=(n_out // tn,),
        in_specs=[pl.BlockSpec((rows, D), lambda j: (0, 0)),
                  pl.BlockSpec((D, tn), lambda j: (0, j)),
                  pl.BlockSpec((1, tn), lambda j: (0, j))],
        out_specs=pl.BlockSpec((rows, tn), lambda j: (0, j)),
        out_shape=jax.ShapeDtypeStruct((rows, n_out), F32),
        compiler_params=_cparams(("arbitrary",)),
        name="ada",
    )(c_pad, w_ada, b_ada.reshape(1, n_out))
    return out[:B]


def _inproj_kernel(x_ref, mod_ref, g_ref, w_ref, cw_ref, cg_ref, qg_ref, kvg_ref,
                   mc_ref, qln_ref, kvln_ref, kpe_ref, cu_scr, *, tm, tiles_per_seq):
    i = pl.program_id(0)
    x = x_ref[...]
    mod = mod_ref[0]
    sh, sc = mod[0:1], mod[1:2]
    h = (x * _rms(x, D_MODEL) * g_ref[...]) * (1.0 + sc) + sh
    hb = h.astype(BF16)

    def proj(lo, hi):
        return _dot(hb, w_ref[:, lo:hi])

    cu = proj(_C_C, _C_U) * proj(_C_U, _C_Q)

    @pl.when(i % tiles_per_seq == 0)
    def _():
        cu_scr[0:8, :] = jnp.zeros((8, CONV_DIM), F32)

    cu_scr[8:8 + tm, :] = cu
    s1 = cu_scr[7:7 + tm, :]
    s2 = cu_scr[6:6 + tm, :]
    cw = cw_ref[...]
    conv = cw[0:1] * s2 + cw[1:2] * s1 + cw[2:3] * cu
    cu_scr[0:8, :] = cu_scr[tm:tm + 8, :]
    y = proj(_C_B, _C_C) * conv
    cg = cg_ref[...]
    for g in range(CONV_GROUPS):
        sl = slice(g * LANES, (g + 1) * LANES)
        yg = y[:, sl]
        mc_ref[:, sl] = (yg * _rms(yg, LANES) * cg[:, sl]).astype(BF16)

    ql = proj(_C_Q, _C_KV)
    qln_ref[...] = (ql * _rms(ql, Q_LORA) * qg_ref[...]).astype(BF16)
    kvl = proj(_C_KV, _C_PE)
    kvln_ref[...] = (kvl * _rms(kvl, KV_LORA) * kvg_ref[...]).astype(BF16)
    kpe_ref[...] = proj(_C_PE, D_IN_PAD)


def _inproj(x2, mod3, norm_g, w_in_p, conv_w, conv_g, q_lat_g, kv_lat_g, S, tm):
    T, D = x2.shape
    tps = S // tm
    row = lambda n: pl.BlockSpec((tm, n), lambda i: (i, 0))
    return pl.pallas_call(
        functools.partial(_inproj_kernel, tm=tm, tiles_per_seq=tps),
        grid=(T // tm,),
        in_specs=[row(D),
                  pl.BlockSpec((1, 6, D), lambda i: (i // tps, 0, 0)),
                  _resident((1, D)),
                  _resident((D, D_IN_PAD)),
                  _resident((CONV_K, CONV_DIM)),
                  _resident((1, CONV_DIM)),
                  _resident((1, Q_LORA)),
                  _resident((1, KV_LORA))],
        out_specs=[row(CONV_DIM), row(Q_LORA), row(KV_LORA), row(LANES)],
        out_shape=[jax.ShapeDtypeStruct((T, CONV_DIM), BF16),
                   jax.ShapeDtypeStruct((T, Q_LORA), BF16),
                   jax.ShapeDtypeStruct((T, KV_LORA), BF16),
                   jax.ShapeDtypeStruct((T, LANES), F32)],
        scratch_shapes=[pltpu.VMEM((tm + 8, CONV_DIM), F32)],
        compiler_params=_cparams(("arbitrary",)),
        name="inproj",
    )(x2, mod3, norm_g, w_in_p, conv_w, conv_g, q_lat_g, kv_lat_g)


def _upproj_kernel(qln_ref, kvln_ref, kpe_ref, pos_ref, invf_ref, wuq_ref, wukv_ref,
                   qg_ref, kg_ref, q_ref, k_ref, v_ref, *, tm):
    ang = pos_ref[...].astype(F32) * invf_ref[...]
    lane = lax.broadcasted_iota(jnp.int32, (tm, LANES), 1)
    cos = jnp.cos(ang)
    sin_signed = jnp.where(lane < LANES // 2, -1.0, 1.0) * jnp.sin(ang)
    even_head = (lane // (QK_ROPE // 2)) % 2 == 0

    def rope(col):
        return col * cos + pltpu.roll(col, LANES // 2, axis=1) * sin_signed

    qg = qg_ref[...]
    kg = kg_ref[...]
    qscale = QK_HEAD ** -0.5 * LOG2E

    q = _dot(qln_ref[...], wuq_ref[...])
    for j in range(MLA_HEADS // 2):
        rc = q[:, MLA_DIM + j * LANES:MLA_DIM + (j + 1) * LANES]
        rc2 = rc * rc
        ss_even = jnp.sum(jnp.where(even_head, rc2, 0.0), axis=-1, keepdims=True)
        ss_odd = jnp.sum(jnp.where(even_head, 0.0, rc2), axis=-1, keepdims=True)
        r = []
        for h, ss_rope in ((2 * j, ss_even), (2 * j + 1, ss_odd)):
            qn = q[:, h * LANES:(h + 1) * LANES]
            ss = jnp.sum(qn * qn, axis=-1, keepdims=True) + ss_rope
            rh = lax.rsqrt(ss * (1.0 / QK_HEAD) + EPS) * qscale
            r.append(rh)
            q_ref[0, h, :, 0:LANES] = (qn * rh * qg[:, 0:LANES]).astype(BF16)
        roped = rope(rc * jnp.where(even_head, r[0], r[1]) * qg[:, LANES:2 * LANES]).astype(BF16)
        q_ref[0, 2 * j, :, LANES:2 * LANES] = roped
        q_ref[0, 2 * j + 1, :, LANES:2 * LANES] = roped

    kv = _dot(kvln_ref[...], wukv_ref[...])
    kp = kpe_ref[...]
    ss_pe = 0.5 * jnp.sum(kp * kp, axis=-1, keepdims=True)
    kr = rope(kp * kg[:, LANES:2 * LANES])
    for h in range(MLA_HEADS):
        kn = kv[:, h * LANES:(h + 1) * LANES]
        ss = jnp.sum(kn * kn, axis=-1, keepdims=True) + ss_pe
        rh = lax.rsqrt(ss * (1.0 / QK_HEAD) + EPS)
        k_ref[0, h, :, 0:LANES] = (kn * rh * kg[:, 0:LANES]).astype(BF16)
        mine = even_head if h % 2 == 0 else jnp.logical_not(even_head)
        k_ref[0, h, :, LANES:2 * LANES] = jnp.where(mine, kr * rh, 0.0).astype(BF16)
        v_ref[0, h, :, 0:V_HEAD] = kv[:, MLA_DIM + h * LANES:MLA_DIM + (h + 1) * LANES].astype(BF16)
        v_ref[0, h, :, V_HEAD:V_EXT] = jnp.ones((tm, V_EXT - V_HEAD), BF16)


def _upproj(qln, kvln, kpe, pos_col, invf, wuq_p, wukv_p, qg_p, kg_p, B, S, tm):
    T = qln.shape[0]
    tps = S // tm
    row = lambda n: pl.BlockSpec((tm, n), lambda i: (i, 0))
    head = lambda n: pl.BlockSpec((1, MLA_HEADS, tm, n), lambda i: (i // tps, 0, i % tps, 0))
    return pl.pallas_call(
        functools.partial(_upproj_kernel, tm=tm),
        grid=(T // tm,),
        in_specs=[row(Q_LORA), row(KV_LORA), row(LANES), row(1),
                  _resident((1, LANES)),
                  _resident(wuq_p.shape), _resident(wukv_p.shape),
                  _resident((1, 2 * LANES)), _resident((1, 2 * LANES))],
        out_specs=[head(QK_PAD), head(QK_PAD), head(V_EXT)],
        out_shape=[jax.ShapeDtypeStruct((B, MLA_HEADS, S, QK_PAD), BF16),
                   jax.ShapeDtypeStruct((B, MLA_HEADS, S, QK_PAD), BF16),
                   jax.ShapeDtypeStruct((B, MLA_HEADS, S, V_EXT), BF16)],
        compiler_params=_cparams(("arbitrary",)),
        name="upproj",
    )(qln, kvln, kpe, pos_col, invf, wuq_p, wukv_p, qg_p, kg_p)


def _attn_kernel(q_ref, k_ref, v_ref, g_ref, o_ref, m_scr, acc_scr, *, ts, nsub, tk):
    qi = pl.program_id(1)
    tq = ts * nsub
    for a in range(nsub):
        m_scr[a] = jnp.full((ts, LANES), -jnp.inf, F32)
        acc_scr[a] = jnp.zeros((ts, V_EXT), F32)

    def step(a, start, width, masked):
        q = q_ref[0, a * ts:(a + 1) * ts, :]
        k = k_ref[0, pl.ds(start, width), :]
        v = v_ref[0, pl.ds(start, width), :]
        s = _dot_nt(q, k)
        if masked:
            row = lax.broadcasted_iota(jnp.int32, (ts, width), 0) + (qi * tq + a * ts)
            col = lax.broadcasted_iota(jnp.int32, (ts, width), 1) + start
            s = jnp.where(col <= row, s, -jnp.inf)
        m_prev = m_scr[a]
        m_new = jnp.maximum(m_prev, jnp.max(s, axis=-1, keepdims=True))
        p = jnp.exp2(s - jnp.concatenate([m_new] * (width // LANES), axis=1))
        corr = jnp.exp2(m_prev - m_new)
        acc_scr[a] = (jnp.concatenate([corr] * (V_EXT // LANES), axis=1) * acc_scr[a]
                      + _dot(p.astype(BF16), v))
        m_scr[a] = m_new

    def body(j, carry):
        start = pl.multiple_of(j * tk, tk)
        for a in range(nsub):
            step(a, start, tk, False)
        return carry

    lax.fori_loop(0, qi * (tq // tk), body, 0)
    for a in range(nsub):
        for d in range(a + 1):
            step(a, pl.multiple_of(qi * tq + d * ts, ts), ts, d == a)

    for a in range(nsub):
        acc = acc_scr[a]
        o = acc[:, 0:V_HEAD] / acc[:, V_HEAD:V_EXT]
        o_ref[0, a * ts:(a + 1) * ts, :] = (o * _rms(o, V_HEAD) * g_ref[0]).astype(BF16)


def _attention(q, k, v, mla_g, B, S, ts, nsub, tk):
    BH = B * MLA_HEADS
    H = MLA_HEADS
    tq = ts * nsub
    return pl.pallas_call(
        functools.partial(_attn_kernel, ts=ts, nsub=nsub, tk=tk),
        grid=(BH, S // tq),
        in_specs=[pl.BlockSpec((1, tq, QK_PAD), lambda bh, qi: (bh, qi, 0)),
                  pl.BlockSpec((1, S, QK_PAD), lambda bh, qi: (bh, 0, 0)),
                  pl.BlockSpec((1, S, V_EXT), lambda bh, qi: (bh, 0, 0)),
                  pl.BlockSpec((1, 1, V_HEAD), lambda bh, qi: (bh % H, 0, 0))],
        out_specs=pl.BlockSpec((1, tq, V_HEAD), lambda bh, qi: (bh // H, qi, bh % H)),
        out_shape=jax.ShapeDtypeStruct((B, S, MLA_DIM), BF16),
        scratch_shapes=[pltpu.VMEM((nsub, ts, LANES), F32), pltpu.VMEM((nsub, ts, V_EXT), F32)],
        compiler_params=_cparams(("arbitrary", "arbitrary")),
        name="attn",
    )(q.reshape(BH, S, QK_PAD), k.reshape(BH, S, QK_PAD), v.reshape(BH, S, V_EXT),
      mla_g.reshape(H, 1, V_HEAD))


def _outproj_kernel(mc_ref, mm_ref, x_ref, mod_ref, w_ref, g_ref, rw_ref, rb_ref,
                    x1_ref, h2_ref, idx_ref, gate_ref, *, tm):
    mod = mod_ref[0]
    g_m, sh_f, sc_f = mod[2:3], mod[3:4], mod[4:5]
    mix = _dot(mc_ref[...], w_ref[0:CONV_DIM, :]) + _dot(mm_ref[...], w_ref[CONV_DIM:, :])
    x1 = x_ref[...] + g_m * mix
    x1_ref[...] = x1
    h2 = (x1 * _rms(x1, D_MODEL) * g_ref[...]) * (1.0 + sc_f) + sh_f
    h2_ref[...] = h2

    h_hi, h_lo = _split_bf16(h2)
    r_hi, r_lo = _split_bf16(rw_ref[...])
    logits = _dot_nt(r_hi, h_hi) + _dot_nt(r_lo, h_hi) + _dot_nt(r_hi, h_lo) + rb_ref[...]
    eid = lax.broadcasted_iota(jnp.int32, (N_EXPERTS, tm), 0)
    vals = []
    for kk in range(TOP_K):
        mx = jnp.max(logits, axis=0, keepdims=True)
        idx = jnp.min(jnp.where(logits == mx, eid, N_EXPERTS), axis=0, keepdims=True)
        idx_ref[kk:kk + 1, :] = idx
        vals.append(mx)
        logits = jnp.where(eid == idx, -jnp.inf, logits)
    ex = [jnp.exp(vv - vals[0]) for vv in vals]
    denom = ex[0] + ex[1] + ex[2] + ex[3]
    for kk in range(TOP_K):
        gate_ref[kk:kk + 1, :] = ex[kk] / denom


def _outproj(mc, mm, x2, mod3, w_out_b, norm_g, rw_t, rb_col, S, tm):
    T, D = x2.shape
    tps = S // tm
    row = lambda n: pl.BlockSpec((tm, n), lambda i: (i, 0))
    colblk = lambda n: pl.BlockSpec((n, tm), lambda i: (0, i))
    return pl.pallas_call(
        functools.partial(_outproj_kernel, tm=tm),
        grid=(T // tm,),
        in_specs=[row(CONV_DIM), row(MLA_DIM), row(D),
                  pl.BlockSpec((1, 6, D), lambda i: (i // tps, 0, 0)),
                  _resident((D, D)), _resident((1, D)),
                  _resident((N_EXPERTS, D)), _resident((N_EXPERTS, 1))],
        out_specs=[row(D), row(D), colblk(TOP_K), colblk(TOP_K)],
        out_shape=[jax.ShapeDtypeStruct((T, D), F32),
                   jax.ShapeDtypeStruct((T, D), F32),
                   jax.ShapeDtypeStruct((TOP_K, T), jnp.int32),
                   jax.ShapeDtypeStruct((TOP_K, T), F32)],
        compiler_params=_cparams(("arbitrary",)),
        name="outproj",
    )(mc, mm, x2, mod3, w_out_b, norm_g, rw_t, rb_col)


def _row_copy(src_hbm, src_row, dst, sem):
    return pltpu.make_async_copy(src_hbm.at[pl.ds(src_row, 1)], dst, sem)


ROW_ISSUE_UNROLL = 8


def _start_rows(idx_at, n_rows, src_hbm, dst, sem):
    def issue(r, c):
        _row_copy(src_hbm, idx_at(r), dst.at[pl.ds(r, 1)], sem).start()
        return c

    lax.fori_loop(0, n_rows, issue, 0, unroll=ROW_ISSUE_UNROLL)


def _wait_rows(n_rows, src_hbm, dst, sem):
    pltpu.make_async_copy(src_hbm.at[pl.ds(0, n_rows)], dst, sem).wait()


W_CHUNK = 64
W_SLOTS = 4


def _load_expert(e, wgu_hbm, wd_hbm, wgu_s, wd_s, stage_gu, stage_d, wsem):
    chunks = ([(wgu_hbm, wgu_s, stage_gu, 0, c) for c in range(wgu_s.shape[0] // W_CHUNK)]
              + [(wd_hbm, wd_s, stage_d, W_SLOTS, c) for c in range(wd_s.shape[0] // W_CHUNK)])

    def copy(item):
        w_hbm, _, stage, sem0, c = item
        return pltpu.make_async_copy(w_hbm.at[e, pl.ds(c * W_CHUNK, W_CHUNK)], stage.at[c % W_SLOTS],
                                     wsem.at[sem0 + c % W_SLOTS])

    for item in chunks[:W_SLOTS - 1]:
        copy(item).start()
    for n, item in enumerate(chunks):
        if n + W_SLOTS - 1 < len(chunks):
            copy(chunks[n + W_SLOTS - 1]).start()
        copy(item).wait()
        _, w_s, stage, _, c = item
        w_s[c * W_CHUNK:(c + 1) * W_CHUNK, :] = stage[c % W_SLOTS].astype(BF16)


def _moe_kernel(te_ref, nu_ref, tok_ref, tok_next_ref, dst_prev_ref, h_hbm, wgu_hbm, bgu_ref, wd_hbm, bd_ref,
                y_hbm, xbuf, sem, wgu_s, wd_s, stage_gu, stage_d, wsem, obuf, osem, *, tme, fc):
    i = pl.program_id(0)
    n_used = nu_ref[0]
    slot = i % 2

    def send_row(r, src_slot):
        return pltpu.make_async_copy(obuf.at[src_slot, pl.ds(r, 1)],
                                     y_hbm.at[pl.ds(dst_prev_ref[0, 0, r], 1)], osem.at[src_slot])

    def wait_sent(src_slot):
        pltpu.make_async_copy(obuf.at[src_slot], y_hbm.at[pl.ds(0, tme)], osem.at[src_slot]).wait()

    @pl.when(i == 0)
    def _():
        _start_rows(lambda r: tok_ref[0, 0, r], tme, h_hbm, xbuf.at[0], sem.at[0])
        obuf[1] = jnp.zeros(obuf.shape[1:], U32)
        pltpu.make_async_copy(obuf.at[1], y_hbm.at[pl.ds((y_hbm.shape[0] // tme - 1) * tme, tme)], osem.at[0]).start()

    e_cur = te_ref[i]
    changed = jnp.logical_or(i == 0, e_cur != te_ref[jnp.maximum(i - 1, 0)])

    @pl.when(jnp.logical_and(i < n_used, changed))
    def _():
        _load_expert(e_cur, wgu_hbm, wd_hbm, wgu_s, wd_s, stage_gu, stage_d, wsem)

    @pl.when(i < n_used)
    def _():
        _wait_rows(tme, h_hbm, xbuf.at[slot], sem.at[slot])
        x = xbuf[slot].astype(BF16)
        acts = []
        n_chunks = D_EXPERT // fc
        per = 2 * tme // n_chunks
        for c in range(n_chunks):
            for r in range((c % (n_chunks // 2)) * per, (c % (n_chunks // 2) + 1) * per):
                if c < n_chunks // 2:
                    _row_copy(h_hbm, tok_next_ref[0, 0, r], xbuf.at[1 - slot, pl.ds(r, 1)], sem.at[1 - slot]).start()
                else:
                    send_row(r, 1 - slot).start()
            g = _dot(x, wgu_s[:, c * fc:(c + 1) * fc]) + bgu_ref[0, :, c * fc:(c + 1) * fc]
            u = (_dot(x, wgu_s[:, D_EXPERT + c * fc:D_EXPERT + (c + 1) * fc])
                 + bgu_ref[0, :, D_EXPERT + c * fc:D_EXPERT + (c + 1) * fc])
            g = jnp.minimum(g, SWIGLU_LIMIT)
            u = jnp.clip(u, -SWIGLU_LIMIT, SWIGLU_LIMIT)
            acts.append(((u + 1.0) * g * jax.nn.sigmoid(SWIGLU_ALPHA * g)).astype(BF16))
        y = _dot(jnp.concatenate(acts, axis=1), wd_s[...]) + bd_ref[0]
        wait_sent(slot)
        obuf[slot] = _pack_halves(y)

    @pl.when(i == n_used)
    def _():
        _wait_rows(tme, h_hbm, xbuf.at[slot], sem.at[slot])
        wait_sent(slot)

        def send(r, c):
            send_row(r, 1 - slot).start()
            return c

        lax.fori_loop(0, tme, send, 0, unroll=ROW_ISSUE_UNROLL)
        wait_sent(1 - slot)
        obuf[slot] = jnp.zeros(obuf.shape[1:], U32)

    @pl.when(i >= n_used)
    def _():
        zs = n_used % 2
        fill = pltpu.make_async_copy(obuf.at[zs], y_hbm.at[pl.ds(pl.multiple_of(i * tme, tme), tme)], osem.at[zs])
        fill.start()
        fill.wait()


def _moe(tile_expert, n_used, buf_tok, dst_rows, h2, w_gu, b_gu, w_d, b_d, tme):
    n_tiles = tile_expert.shape[0]
    D = h2.shape[1]
    tok3 = buf_tok.reshape(n_tiles, 1, tme)
    spare = n_tiles * tme + jnp.arange(tme, dtype=jnp.int32)
    dst3 = jnp.concatenate([spare, dst_rows]).reshape(n_tiles + 1, 1, tme)
    grid_spec = pltpu.PrefetchScalarGridSpec(
        num_scalar_prefetch=2,
        grid=(n_tiles,),
        in_specs=[
            pl.BlockSpec((1, 1, tme), lambda i, te, nu: (i, 0, 0), memory_space=pltpu.SMEM),
            pl.BlockSpec((1, 1, tme), lambda i, te, nu: (jnp.minimum(i + 1, n_tiles - 1), 0, 0),
                         memory_space=pltpu.SMEM),
            pl.BlockSpec((1, 1, tme), lambda i, te, nu: (i, 0, 0), memory_space=pltpu.SMEM),
            pl.BlockSpec(memory_space=pl.ANY),
            pl.BlockSpec(memory_space=pl.ANY),
            pl.BlockSpec((1, 1, 2 * D_EXPERT), lambda i, te, nu: (te[i], 0, 0)),
            pl.BlockSpec(memory_space=pl.ANY),
            pl.BlockSpec((1, 1, D), lambda i, te, nu: (te[i], 0, 0)),
        ],
        out_specs=pl.BlockSpec(memory_space=pl.ANY),
        scratch_shapes=[pltpu.VMEM((2, tme, D), F32), pltpu.SemaphoreType.DMA((2,)),
                        pltpu.VMEM((D, 2 * D_EXPERT), BF16), pltpu.VMEM((D_EXPERT, D), BF16),
                        pltpu.VMEM((W_SLOTS, W_CHUNK, 2 * D_EXPERT), F32), pltpu.VMEM((W_SLOTS, W_CHUNK, D), F32),
                        pltpu.SemaphoreType.DMA((2 * W_SLOTS,)),
                        pltpu.VMEM((2, tme, D // 2), U32), pltpu.SemaphoreType.DMA((2,))],
    )
    return pl.pallas_call(
        functools.partial(_moe_kernel, tme=tme, fc=256),
        grid_spec=grid_spec,
        out_shape=jax.ShapeDtypeStruct(((n_tiles + 2) * tme, D // 2), U32),
        compiler_params=_cparams(("arbitrary",)),
        name="moe",
    )(tile_expert, n_used, tok3, tok3, dst3, h2,
      w_gu, b_gu.reshape(N_EXPERTS, 1, 2 * D_EXPERT), w_d, b_d.reshape(N_EXPERTS, 1, D))


def _combine_kernel(y0_ref, y1_ref, y2_ref, y3_ref, gate_ref, x1_ref, mod_ref, o_ref):
    gates = gate_ref[...]
    n = o_ref.shape[1] // 2
    y_lo = y_hi = None
    for kk, y_ref in enumerate((y0_ref, y1_ref, y2_ref, y3_ref)):
        lo, hi = _unpack_halves(y_ref[...])
        g = gates[:, kk:kk + 1]
        y_lo = g * lo if y_lo is None else y_lo + g * lo
        y_hi = g * hi if y_hi is None else y_hi + g * hi
    g_f = mod_ref[0][5:6]
    o_ref[:, 0:n] = x1_ref[:, 0:n] + g_f[:, 0:n] * y_lo
    o_ref[:, n:] = x1_ref[:, n:] + g_f[:, n:] * y_hi


def _combine(y_slots, gates_t, x1, mod3, S, tm):
    T, D = x1.shape
    tps = S // tm
    n_tiles = T // tm
    slot_spec = lambda kk: pl.BlockSpec((tm, D // 2), lambda i: (kk * n_tiles + i, 0))
    return pl.pallas_call(
        _combine_kernel,
        grid=(n_tiles,),
        in_specs=[slot_spec(0), slot_spec(1), slot_spec(2), slot_spec(3),
                  pl.BlockSpec((tm, TOP_K), lambda i: (i, 0)),
                  pl.BlockSpec((tm, D), lambda i: (i, 0)),
                  pl.BlockSpec((1, 6, D), lambda i: (i // tps, 0, 0))],
        out_specs=pl.BlockSpec((tm, D), lambda i: (i, 0)),
        out_shape=jax.ShapeDtypeStruct((T, D), F32),
        compiler_params=_cparams(("arbitrary",)),
        name="combine",
    )(y_slots, y_slots, y_slots, y_slots, gates_t, x1, mod3)


def _route(top_i, tme):
    T = top_i.shape[1]
    TK = T * TOP_K
    assert TK % tme == 0
    flat_e = top_i.T.reshape(-1)
    counts = jnp.sum((flat_e[:, None] == jnp.arange(N_EXPERTS, dtype=jnp.int32)[None, :]).astype(jnp.int32), axis=0)
    padded = ((counts + tme - 1) // tme) * tme
    padded_end = jnp.cumsum(padded)
    padded_start = padded_end - padded
    group_end = jnp.cumsum(counts)
    group_start = group_end - counts

    n_tiles = TK // tme + N_EXPERTS
    tile_start = jnp.arange(n_tiles) * tme
    tile_expert = jnp.minimum(jnp.sum(tile_start[:, None] >= padded_end[None, :], axis=1),
                              N_EXPERTS - 1).astype(jnp.int32)
    n_used = (padded_end[-1] // tme).astype(jnp.int32).reshape(1)

    a_sorted = jnp.argsort(flat_e).astype(jnp.int32)
    p = tile_start[:, None] + jnp.arange(tme)[None, :]
    r = p - padded_start[tile_expert][:, None]
    valid = r < counts[tile_expert][:, None]
    src = jnp.clip(group_start[tile_expert][:, None] + r, 0, TK - 1)
    a = a_sorted[src.reshape(-1)].reshape(n_tiles, tme)
    buf_tok = jnp.where(valid, a // TOP_K, 0).astype(jnp.int32).reshape(-1)
    dst_rows = jnp.where(valid, (a % TOP_K) * T + a // TOP_K,
                         TK + p - group_end[tile_expert][:, None]).astype(jnp.int32).reshape(-1)
    return tile_expert, n_used, buf_tok, dst_rows


def _permute_weights(w_in, w_uq, w_ukv, q_head_g, k_head_g):
    half = QK_ROPE // 2
    pe0 = _C_PE
    pe_cols = jnp.concatenate([jnp.arange(pe0, pe0 + half), jnp.arange(pe0, pe0 + half),
                               jnp.arange(pe0 + half, pe0 + 2 * half), jnp.arange(pe0 + half, pe0 + 2 * half)])
    w_in_p = jnp.concatenate([w_in[:, :pe0], w_in[:, pe_cols]], axis=1).astype(BF16)

    hq = jnp.arange(MLA_HEADS)[:, None] * QK_HEAD
    nope_cols = (hq + jnp.arange(QK_NOPE)[None, :]).reshape(-1)
    rope_cols = []
    for j in range(MLA_HEADS // 2):
        for part in range(2):
            for h in (2 * j, 2 * j + 1):
                rope_cols.append(h * QK_HEAD + QK_NOPE + part * half + jnp.arange(half))
    wuq_p = w_uq[:, jnp.concatenate([nope_cols] + rope_cols)].astype(BF16)

    hk = jnp.arange(MLA_HEADS)[:, None] * (QK_NOPE + V_HEAD)
    kn_cols = (hk + jnp.arange(QK_NOPE)[None, :]).reshape(-1)
    v_cols = (hk + QK_NOPE + jnp.arange(V_HEAD)[None, :]).reshape(-1)
    wukv_p = w_ukv[:, jnp.concatenate([kn_cols, v_cols])].astype(BF16)

    def gain_p(g):
        x1, x2 = g[QK_NOPE:QK_NOPE + half], g[QK_NOPE + half:]
        return jnp.concatenate([g[:QK_NOPE], x1, x1, x2, x2]).reshape(1, 2 * LANES)

    return w_in_p, wuq_p, wukv_p, gain_p(q_head_g), gain_p(k_head_g)


def _tile(n, pref):
    t = pref
    while n % t:
        t //= 2
    return t


def kernel(x, c, positions, w_ada, b_ada, norm_mix_g, w_in, conv_w, q_lat_norm_g, w_uq, kv_lat_norm_g, w_ukv, q_head_norm_g, k_head_norm_g, conv_out_norm_g, mla_out_norm_g, w_out, norm_ffn_g, router_w, router_b, w_gate_up, b_gate_up, w_down, b_down):
    B, S, D = x.shape
    T = B * S
    assert D == D_MODEL and w_ada.shape[0] == 1 and S % 128 == 0
    l = 0
    tm = _tile(S, 512)
    ts = 512 if S >= 8192 else S // 8
    tme = 512 if T * TOP_K >= 32768 else 256
    tmc = _tile(S, 256)

    mod3 = _ada(c, w_ada[l], b_ada[l]).reshape(B, 6, D)
    x2 = x.reshape(T, D)

    w_in_p, wuq_p, wukv_p, qg_p, kg_p = _permute_weights(
        w_in[l], w_uq[l], w_ukv[l], q_head_norm_g[l], k_head_norm_g[l])
    half = QK_ROPE // 2
    inv_freq = ROPE_THETA ** (-jnp.arange(half, dtype=F32) / half)
    invf = jnp.tile(inv_freq, 4).reshape(1, LANES)

    mc, qln, kvln, kpe = _inproj(
        x2, mod3, norm_mix_g[l].reshape(1, D), w_in_p, conv_w[l],
        conv_out_norm_g[l].reshape(1, CONV_DIM), q_lat_norm_g[l].reshape(1, Q_LORA),
        kv_lat_norm_g[l].reshape(1, KV_LORA), S, tm)
    q, k, v = _upproj(qln, kvln, kpe, positions.reshape(T, 1), invf, wuq_p, wukv_p, qg_p, kg_p, B, S, tm)
    mm = _attention(q, k, v, mla_out_norm_g[l], B, S, ts, 4, 4 * ts).reshape(T, MLA_DIM)
    x1, h2, top_i, gates = _outproj(
        mc, mm, x2, mod3, w_out[l].astype(BF16), norm_ffn_g[l].reshape(1, D),
        router_w[l].T, router_b[l].reshape(N_EXPERTS, 1), S, tm)

    tile_expert, n_used, buf_tok, dst_rows = _route(top_i, tme)
    y_slots = _moe(tile_expert, n_used, buf_tok, dst_rows, h2, w_gate_up[l], b_gate_up[l], w_down[l], b_down[l], tme)
    out = _combine(y_slots, gates.T, x1, mod3, S, tmc)
    return out.reshape(B, S, D)
```

```python
import functools

import jax
import jax.numpy as jnp
from jax import lax
from jax.experimental import pallas as pl
from jax.experimental.pallas import tpu as pltpu

F32 = jnp.float32
BF16 = jnp.bfloat16
U32 = jnp.uint32

D_MODEL = 2048
CONV_GROUPS = 8
CONV_DIM = 1024
CONV_K = 3
MLA_HEADS = 8
QK_NOPE = 128
QK_ROPE = 64
QK_HEAD = QK_NOPE + QK_ROPE
V_HEAD = 128
MLA_DIM = MLA_HEADS * V_HEAD
Q_LORA = 768
KV_LORA = 512
ROPE_THETA = 10000.0
N_EXPERTS = 32
TOP_K = 4
D_EXPERT = D_MODEL
SWIGLU_LIMIT = 7.0
SWIGLU_ALPHA = 1.702
EPS = 1e-6

LANES = 128
QK_PAD = 2 * LANES
V_EXT = 2 * LANES
LOG2E = 1.4426950408889634
_C_B, _C_C, _C_U = 0, CONV_DIM, 2 * CONV_DIM
_C_Q = 3 * CONV_DIM
_C_KV = _C_Q + Q_LORA
_C_PE = _C_KV + KV_LORA
D_IN_PAD = _C_PE + LANES
VMEM_LIMIT = 56 * 1024 * 1024


def _cparams(sem):
    return pltpu.CompilerParams(dimension_semantics=sem, vmem_limit_bytes=VMEM_LIMIT)


def _resident(shape):
    nd = len(shape)
    return pl.BlockSpec(shape, lambda *_: (0,) * nd, pipeline_mode=pl.Buffered(1))


def _split_bf16(a):
    hi = a.astype(BF16)
    lo = (a - hi.astype(F32)).astype(BF16)
    return hi, lo


def _dot(a, b):
    return jnp.dot(a, b, preferred_element_type=F32)


def _dot_nt(a, b):
    return lax.dot_general(a, b, (((1,), (1,)), ((), ())), preferred_element_type=F32)


def _pack_halves(y):
    n = y.shape[1] // 2
    bits = lambda v: lax.bitcast_convert_type(v.astype(BF16).astype(F32), U32)
    return bits(y[:, n:]) | (bits(y[:, :n]) >> 16)


def _unpack_halves(w):
    lo = lax.bitcast_convert_type(w << 16, F32)
    hi = lax.bitcast_convert_type(w & jnp.uint32(0xFFFF0000), F32)
    return lo, hi


def _rms(x, n):
    return lax.rsqrt(jnp.sum(x * x, axis=-1, keepdims=True) * (1.0 / n) + EPS)


def _ada_kernel(c_ref, w_ref, b_ref, o_ref):
    c = c_ref[...]
    cond = c * jax.nn.sigmoid(c)
    c_hi, c_lo = _split_bf16(cond)
    w_hi, w_lo = _split_bf16(w_ref[...])
    o_ref[...] = _dot(c_hi, w_hi) + _dot(c_hi, w_lo) + _dot(c_lo, w_hi) + b_ref[...]


def _ada(c, w_ada, b_ada):
    B, D = c.shape
    n_out = w_ada.shape[1]
    rows = 8
    c_pad = jnp.zeros((rows, D), F32).at[:B].set(c)
    tn = 1024
    out = pl.pallas_call(
        _ada_kernel,
        grid=(n_out // tn,),
        in_specs=[pl.BlockSpec((rows, D), lambda j: (0, 0)),
                  pl.BlockSpec((D, tn), lambda j: (0, j)),
                  pl.BlockSpec((1, tn), lambda j: (0, j))],
        out_specs=pl.BlockSpec((rows, tn), lambda j: (0, j)),
        out_shape=jax.ShapeDtypeStruct((rows, n_out), F32),
        compiler_params=_cparams(("arbitrary",)),
        name="ada",
    )(c_pad, w_ada, b_ada.reshape(1, n_out))
    return out[:B]


def _inproj_kernel(x_ref, mod_ref, g_ref, w_ref, cw_ref, cg_ref, qg_ref, kvg_ref,
                   mc_ref, qln_ref, kvln_ref, kpe_ref, cu_scr, *, tm, tiles_per_seq):
    i = pl.program_id(0)
    x = x_ref[...]
    mod = mod_ref[0]
    sh, sc = mod[0:1], mod[1:2]
    h = (x * _rms(x, D_MODEL) * g_ref[...]) * (1.0 + sc) + sh
    hb = h.astype(BF16)

    def proj(lo, hi):
        return _dot(hb, w_ref[:, lo:hi])

    cu = proj(_C_C, _C_U) * proj(_C_U, _C_Q)

    @pl.when(i % tiles_per_seq == 0)
    def _():
        cu_scr[0:8, :] = jnp.zeros((8, CONV_DIM), F32)

    cu_scr[8:8 + tm, :] = cu
    s1 = cu_scr[7:7 + tm, :]
    s2 = cu_scr[6:6 + tm, :]
    cw = cw_ref[...]
    conv = cw[0:1] * s2 + cw[1:2] * s1 + cw[2:3] * cu
    cu_scr[0:8, :] = cu_scr[tm:tm + 8, :]
    y = proj(_C_B, _C_C) * conv
    cg = cg_ref[...]
    for g in range(CONV_GROUPS):
        sl = slice(g * LANES, (g + 1) * LANES)
        yg = y[:, sl]
        mc_ref[:, sl] = (yg * _rms(yg, LANES) * cg[:, sl]).astype(BF16)

    ql = proj(_C_Q, _C_KV)
    qln_ref[...] = (ql * _rms(ql, Q_LORA) * qg_ref[...]).astype(BF16)
    kvl = proj(_C_KV, _C_PE)
    kvln_ref[...] = (kvl * _rms(kvl, KV_LORA) * kvg_ref[...]).astype(BF16)
    kpe_ref[...] = proj(_C_PE, D_IN_PAD)


def _inproj(x2, mod3, norm_g, w_in_p, conv_w, conv_g, q_lat_g, kv_lat_g, S, tm):
    T, D = x2.shape
    tps = S // tm
    row = lambda n: pl.BlockSpec((tm, n), lambda i: (i, 0))
    return pl.pallas_call(
        functools.partial(_inproj_kernel, tm=tm, tiles_per_seq=tps),
        grid=(T // tm,),
        in_specs=[row(D),
                  pl.BlockSpec((1, 6, D), lambda i: (i // tps, 0, 0)),
                  _resident((1, D)),
                  _resident((D, D_IN_PAD)),
                  _resident((CONV_K, CONV_DIM)),
                  _resident((1, CONV_DIM)),
                  _resident((1, Q_LORA)),
                  _resident((1, KV_LORA))],
        out_specs=[row(CONV_DIM), row(Q_LORA), row(KV_LORA), row(LANES)],
        out_shape=[jax.ShapeDtypeStruct((T, CONV_DIM), BF16),
                   jax.ShapeDtypeStruct((T, Q_LORA), BF16),
                   jax.ShapeDtypeStruct((T, KV_LORA), BF16),
                   jax.ShapeDtypeStruct((T, LANES), F32)],
        scratch_shapes=[pltpu.VMEM((tm + 8, CONV_DIM), F32)],
        compiler_params=_cparams(("arbitrary",)),
        name="inproj",
    )(x2, mod3, norm_g, w_in_p, conv_w, conv_g, q_lat_g, kv_lat_g)


def _upproj_kernel(qln_ref, kvln_ref, kpe_ref, pos_ref, invf_ref, wuq_ref, wukv_ref,
                   qg_ref, kg_ref, q_ref, k_ref, v_ref, *, tm):
    ang = pos_ref[...].astype(F32) * invf_ref[...]
    lane = lax.broadcasted_iota(jnp.int32, (tm, LANES), 1)
    cos = jnp.cos(ang)
    sin_signed = jnp.where(lane < LANES // 2, -1.0, 1.0) * jnp.sin(ang)
    even_head = (lane // (QK_ROPE // 2)) % 2 == 0

    def rope(col):
        return col * cos + pltpu.roll(col, LANES // 2, axis=1) * sin_signed

    qg = qg_ref[...]
    kg = kg_ref[...]
    qscale = QK_HEAD ** -0.5 * LOG2E

    q = _dot(qln_ref[...], wuq_ref[...])
    for j in range(MLA_HEADS // 2):
        rc = q[:, MLA_DIM + j * LANES:MLA_DIM + (j + 1) * LANES]
        rc2 = rc * rc
        ss_even = jnp.sum(jnp.where(even_head, rc2, 0.0), axis=-1, keepdims=True)
        ss_odd = jnp.sum(jnp.where(even_head, 0.0, rc2), axis=-1, keepdims=True)
        r = []
        for h, ss_rope in ((2 * j, ss_even), (2 * j + 1, ss_odd)):
            qn = q[:, h * LANES:(h + 1) * LANES]
            ss = jnp.sum(qn * qn, axis=-1, keepdims=True) + ss_rope
            rh = lax.rsqrt(ss * (1.0 / QK_HEAD) + EPS) * qscale
            r.append(rh)
            q_ref[0, h, :, 0:LANES] = (qn * rh * qg[:, 0:LANES]).astype(BF16)
        roped = rope(rc * jnp.where(even_head, r[0], r[1]) * qg[:, LANES:2 * LANES]).astype(BF16)
        q_ref[0, 2 * j, :, LANES:2 * LANES] = roped
        q_ref[0, 2 * j + 1, :, LANES:2 * LANES] = roped

    kv = _dot(kvln_ref[...], wukv_ref[...])
    kp = kpe_ref[...]
    ss_pe = 0.5 * jnp.sum(kp * kp, axis=-1, keepdims=True)
    kr = rope(kp * kg[:, LANES:2 * LANES])
    for h in range(MLA_HEADS):
        kn = kv[:, h * LANES:(h + 1) * LANES]
        ss = jnp.sum(kn * kn, axis=-1, keepdims=True) + ss_pe
        rh = lax.rsqrt(ss * (1.0 / QK_HEAD) + EPS)
        k_ref[0, h, :, 0:LANES] = (kn * rh * kg[:, 0:LANES]).astype(BF16)
        mine = even_head if h % 2 == 0 else jnp.logical_not(even_head)
        k_ref[0, h, :, LANES:2 * LANES] = jnp.where(mine, kr * rh, 0.0).astype(BF16)
        v_ref[0, h, :, 0:V_HEAD] = kv[:, MLA_DIM + h * LANES:MLA_DIM + (h + 1) * LANES].astype(BF16)
        v_ref[0, h, :, V_HEAD:V_EXT] = jnp.ones((tm, V_EXT - V_HEAD), BF16)


def _upproj(qln, kvln, kpe, pos_col, invf, wuq_p, wukv_p, qg_p, kg_p, B, S, tm):
    T = qln.shape[0]
    tps = S // tm
    row = lambda n: pl.BlockSpec((tm, n), lambda i: (i, 0))
    head = lambda n: pl.BlockSpec((1, MLA_HEADS, tm, n), lambda i: (i // tps, 0, i % tps, 0))
    return pl.pallas_call(
        functools.partial(_upproj_kernel, tm=tm),
        grid=(T // tm,),
        in_specs=[row(Q_LORA), row(KV_LORA), row(LANES), row(1),
                  _resident((1, LANES)),
                  _resident(wuq_p.shape), _resident(wukv_p.shape),
                  _resident((1, 2 * LANES)), _resident((1, 2 * LANES))],
        out_specs=[head(QK_PAD), head(QK_PAD), head(V_EXT)],
        out_shape=[jax.ShapeDtypeStruct((B, MLA_HEADS, S, QK_PAD), BF16),
                   jax.ShapeDtypeStruct((B, MLA_HEADS, S, QK_PAD), BF16),
                   jax.ShapeDtypeStruct((B, MLA_HEADS, S, V_EXT), BF16)],
        compiler_params=_cparams(("arbitrary",)),
        name="upproj",
    )(qln, kvln, kpe, pos_col, invf, wuq_p, wukv_p, qg_p, kg_p)


def _attn_kernel(q_ref, k_ref, v_ref, g_ref, o_ref, m_scr, acc_scr, *, ts, nsub, tk):
    qi = pl.program_id(1)
    tq = ts * nsub
    for a in range(nsub):
        m_scr[a] = jnp.full((ts, LANES), -jnp.inf, F32)
        acc_scr[a] = jnp.zeros((ts, V_EXT), F32)

    def step(a, start, width, masked):
        q = q_ref[0, a * ts:(a + 1) * ts, :]
        k = k_ref[0, pl.ds(start, width), :]
        v = v_ref[0, pl.ds(start, width), :]
        s = _dot_nt(q, k)
        if masked:
            row = lax.broadcasted_iota(jnp.int32, (ts, width), 0) + (qi * tq + a * ts)
            col = lax.broadcasted_iota(jnp.int32, (ts, width), 1) + start
            s = jnp.where(col <= row, s, -jnp.inf)
        m_prev = m_scr[a]
        m_new = jnp.maximum(m_prev, jnp.max(s, axis=-1, keepdims=True))
        p = jnp.exp2(s - jnp.concatenate([m_new] * (width // LANES), axis=1))
        corr = jnp.exp2(m_prev - m_new)
        acc_scr[a] = (jnp.concatenate([corr] * (V_EXT // LANES), axis=1) * acc_scr[a]
                      + _dot(p.astype(BF16), v))
        m_scr[a] = m_new

    def body(j, carry):
        start = pl.multiple_of(j * tk, tk)
        for a in range(nsub):
            step(a, start, tk, False)
        return carry

    lax.fori_loop(0, qi * (tq // tk), body, 0)
    for a in range(nsub):
        for d in range(a + 1):
            step(a, pl.multiple_of(qi * tq + d * ts, ts), ts, d == a)

    for a in range(nsub):
        acc = acc_scr[a]
        o = acc[:, 0:V_HEAD] / acc[:, V_HEAD:V_EXT]
        o_ref[0, a * ts:(a + 1) * ts, :] = (o * _rms(o, V_HEAD) * g_ref[0]).astype(BF16)


def _attention(q, k, v, mla_g, B, S, ts, nsub, tk):
    BH = B * MLA_HEADS
    H = MLA_HEADS
    tq = ts * nsub
    return pl.pallas_call(
        functools.partial(_attn_kernel, ts=ts, nsub=nsub, tk=tk),
        grid=(BH, S // tq),
        in_specs=[pl.BlockSpec((1, tq, QK_PAD), lambda bh, qi: (bh, qi, 0)),
                  pl.BlockSpec((1, S, QK_PAD), lambda bh, qi: (bh, 0, 0)),
                  pl.BlockSpec((1, S, V_EXT), lambda bh, qi: (bh, 0, 0)),
                  pl.BlockSpec((1, 1, V_HEAD), lambda bh, qi: (bh % H, 0, 0))],
        out_specs=pl.BlockSpec((1, tq, V_HEAD), lambda bh, qi: (bh // H, qi, bh % H)),
        out_shape=jax.ShapeDtypeStruct((B, S, MLA_DIM), BF16),
        scratch_shapes=[pltpu.VMEM((nsub, ts, LANES), F32), pltpu.VMEM((nsub, ts, V_EXT), F32)],
        compiler_params=_cparams(("arbitrary", "arbitrary")),
        name="attn",
    )(q.reshape(BH, S, QK_PAD), k.reshape(BH, S, QK_PAD), v.reshape(BH, S, V_EXT),
      mla_g.reshape(H, 1, V_HEAD))


def _outproj_kernel(mc_ref, mm_ref, x_ref, mod_ref, w_ref, g_ref, rw_ref, rb_ref,
                    x1_ref, h2_ref, idx_ref, gate_ref, *, tm):
    mod = mod_ref[0]
    g_m, sh_f, sc_f = mod[2:3], mod[3:4], mod[4:5]
    mix = _dot(mc_ref[...], w_ref[0:CONV_DIM, :]) + _dot(mm_ref[...], w_ref[CONV_DIM:, :])
    x1 = x_ref[...] + g_m * mix
    x1_ref[...] = x1
    h2 = (x1 * _rms(x1, D_MODEL) * g_ref[...]) * (1.0 + sc_f) + sh_f
    h2_ref[...] = h2

    h_hi, h_lo = _split_bf16(h2)
    r_hi, r_lo = _split_bf16(rw_ref[...])
    logits = _dot_nt(r_hi, h_hi) + _dot_nt(r_lo, h_hi) + _dot_nt(r_hi, h_lo) + rb_ref[...]
    eid = lax.broadcasted_iota(jnp.int32, (N_EXPERTS, tm), 0)
    vals = []
    for kk in range(TOP_K):
        mx = jnp.max(logits, axis=0, keepdims=True)
        idx = jnp.min(jnp.where(logits == mx, eid, N_EXPERTS), axis=0, keepdims=True)
        idx_ref[kk:kk + 1, :] = idx
        vals.append(mx)
        logits = jnp.where(eid == idx, -jnp.inf, logits)
    ex = [jnp.exp(vv - vals[0]) for vv in vals]
    denom = ex[0] + ex[1] + ex[2] + ex[3]
    for kk in range(TOP_K):
        gate_ref[kk:kk + 1, :] = ex[kk] / denom


def _outproj(mc, mm, x2, mod3, w_out_b, norm_g, rw_t, rb_col, S, tm):
    T, D = x2.shape
    tps = S // tm
    row = lambda n: pl.BlockSpec((tm, n), lambda i: (i, 0))
    colblk = lambda n: pl.BlockSpec((n, tm), lambda i: (0, i))
    return pl.pallas_call(
        functools.partial(_outproj_kernel, tm=tm),
        grid=(T // tm,),
        in_specs=[row(CONV_DIM), row(MLA_DIM), row(D),
                  pl.BlockSpec((1, 6, D), lambda i: (i // tps, 0, 0)),
                  _resident((D, D)), _resident((1, D)),
                  _resident((N_EXPERTS, D)), _resident((N_EXPERTS, 1))],
        out_specs=[row(D), row(D), colblk(TOP_K), colblk(TOP_K)],
        out_shape=[jax.ShapeDtypeStruct((T, D), F32),
                   jax.ShapeDtypeStruct((T, D), F32),
                   jax.ShapeDtypeStruct((TOP_K, T), jnp.int32),
                   jax.ShapeDtypeStruct((TOP_K, T), F32)],
        compiler_params=_cparams(("arbitrary",)),
        name="outproj",
    )(mc, mm, x2, mod3, w_out_b, norm_g, rw_t, rb_col)


def _row_copy(src_hbm, src_row, dst, sem):
    return pltpu.make_async_copy(src_hbm.at[pl.ds(src_row, 1)], dst, sem)


ROW_ISSUE_UNROLL = 8


def _start_rows(idx_at, n_rows, src_hbm, dst, sem):
    def issue(r, c):
        _row_copy(src_hbm, idx_at(r), dst.at[pl.ds(r, 1)], sem).start()
        return c

    lax.fori_loop(0, n_rows, issue, 0, unroll=ROW_ISSUE_UNROLL)


def _wait_rows(n_rows, src_hbm, dst, sem):
    pltpu.make_async_copy(src_hbm.at[pl.ds(0, n_rows)], dst, sem).wait()


W_CHUNK = 64
W_SLOTS = 4
W_DMA_PRIORITY = 1


def _load_expert(e, wgu_hbm, wd_hbm, wgu_s, wd_s, stage_gu, stage_d, wsem):
    chunks = ([(wgu_hbm, wgu_s, stage_gu, 0, c) for c in range(wgu_s.shape[0] // W_CHUNK)]
              + [(wd_hbm, wd_s, stage_d, W_SLOTS, c) for c in range(wd_s.shape[0] // W_CHUNK)])

    def copy(item):
        w_hbm, _, stage, sem0, c = item
        return pltpu.make_async_copy(w_hbm.at[e, pl.ds(c * W_CHUNK, W_CHUNK)], stage.at[c % W_SLOTS],
                                     wsem.at[sem0 + c % W_SLOTS])

    for item in chunks[:W_SLOTS - 1]:
        copy(item).start(priority=W_DMA_PRIORITY)
    for n, item in enumerate(chunks):
        if n + W_SLOTS - 1 < len(chunks):
            copy(chunks[n + W_SLOTS - 1]).start(priority=W_DMA_PRIORITY)
        copy(item).wait()
        _, w_s, stage, _, c = item
        w_s[c * W_CHUNK:(c + 1) * W_CHUNK, :] = stage[c % W_SLOTS].astype(BF16)


def _moe_kernel(te_ref, nu_ref, tok_ref, tok_next_ref, dst_prev_ref, h_hbm, wgu_hbm, bgu_ref, wd_hbm, bd_ref,
                y_hbm, xbuf, sem, wgu_s, wd_s, stage_gu, stage_d, wsem, obuf, osem, *, tme, fc):
    i = pl.program_id(0)
    n_used = nu_ref[0]
    slot = i % 2

    def send_row(r, src_slot):
        return pltpu.make_async_copy(obuf.at[src_slot, pl.ds(r, 1)],
                                     y_hbm.at[pl.ds(dst_prev_ref[0, 0, r], 1)], osem.at[src_slot])

    def wait_sent(src_slot):
        pltpu.make_async_copy(obuf.at[src_slot], y_hbm.at[pl.ds(0, tme)], osem.at[src_slot]).wait()

    @pl.when(i == 0)
    def _():
        _start_rows(lambda r: tok_ref[0, 0, r], tme, h_hbm, xbuf.at[0], sem.at[0])
        obuf[1] = jnp.zeros(obuf.shape[1:], U32)
        pltpu.make_async_copy(obuf.at[1], y_hbm.at[pl.ds((y_hbm.shape[0] // tme - 1) * tme, tme)], osem.at[0]).start()

    e_cur = te_ref[i]
    changed = jnp.logical_or(i == 0, e_cur != te_ref[jnp.maximum(i - 1, 0)])

    @pl.when(jnp.logical_and(i < n_used, changed))
    def _():
        _load_expert(e_cur, wgu_hbm, wd_hbm, wgu_s, wd_s, stage_gu, stage_d, wsem)

    @pl.when(i < n_used)
    def _():
        _wait_rows(tme, h_hbm, xbuf.at[slot], sem.at[slot])
        x = xbuf[slot].astype(BF16)
        acts = []
        n_chunks = D_EXPERT // fc
        per = 2 * tme // n_chunks
        for c in range(n_chunks):
            for r in range((c % (n_chunks // 2)) * per, (c % (n_chunks // 2) + 1) * per):
                if c < n_chunks // 2:
                    _row_copy(h_hbm, tok_next_ref[0, 0, r], xbuf.at[1 - slot, pl.ds(r, 1)], sem.at[1 - slot]).start()
                else:
                    send_row(r, 1 - slot).start(priority=r % 2)
            g = _dot(x, wgu_s[:, c * fc:(c + 1) * fc]) + bgu_ref[0, :, c * fc:(c + 1) * fc]
            u = (_dot(x, wgu_s[:, D_EXPERT + c * fc:D_EXPERT + (c + 1) * fc])
                 + bgu_ref[0, :, D_EXPERT + c * fc:D_EXPERT + (c + 1) * fc])
            g = jnp.minimum(g, SWIGLU_LIMIT)
            u = jnp.clip(u, -SWIGLU_LIMIT, SWIGLU_LIMIT)
            acts.append(((u + 1.0) * g * jax.nn.sigmoid(SWIGLU_ALPHA * g)).astype(BF16))
        y = _dot(jnp.concatenate(acts, axis=1), wd_s[...]) + bd_ref[0]
        wait_sent(slot)
        obuf[slot] = _pack_halves(y)

    @pl.when(i == n_used)
    def _():
        _wait_rows(tme, h_hbm, xbuf.at[slot], sem.at[slot])
        wait_sent(slot)

        def send(r, c):
            send_row(r, 1 - slot).start()
            return c

        lax.fori_loop(0, tme, send, 0, unroll=ROW_ISSUE_UNROLL)
        wait_sent(1 - slot)
        obuf[slot] = jnp.zeros(obuf.shape[1:], U32)

    @pl.when(i >= n_used)
    def _():
        zs = n_used % 2
        fill = pltpu.make_async_copy(obuf.at[zs], y_hbm.at[pl.ds(pl.multiple_of(i * tme, tme), tme)], osem.at[zs])
        fill.start()
        fill.wait()


def _moe(tile_expert, n_used, buf_tok, dst_rows, h2, w_gu, b_gu, w_d, b_d, tme):
    n_tiles = tile_expert.shape[0]
    D = h2.shape[1]
    tok3 = buf_tok.reshape(n_tiles, 1, tme)
    spare = n_tiles * tme + jnp.arange(tme, dtype=jnp.int32)
    dst3 = jnp.concatenate([spare, dst_rows]).reshape(n_tiles + 1, 1, tme)
    grid_spec = pltpu.PrefetchScalarGridSpec(
        num_scalar_prefetch=2,
        grid=(n_tiles,),
        in_specs=[
            pl.BlockSpec((1, 1, tme), lambda i, te, nu: (i, 0, 0), memory_space=pltpu.SMEM),
            pl.BlockSpec((1, 1, tme), lambda i, te, nu: (jnp.minimum(i + 1, n_tiles - 1), 0, 0),
                         memory_space=pltpu.SMEM),
            pl.BlockSpec((1, 1, tme), lambda i, te, nu: (i, 0, 0), memory_space=pltpu.SMEM),
            pl.BlockSpec(memory_space=pl.ANY),
            pl.BlockSpec(memory_space=pl.ANY),
            pl.BlockSpec((1, 1, 2 * D_EXPERT), lambda i, te, nu: (te[i], 0, 0)),
            pl.BlockSpec(memory_space=pl.ANY),
            pl.BlockSpec((1, 1, D), lambda i, te, nu: (te[i], 0, 0)),
        ],
        out_specs=pl.BlockSpec(memory_space=pl.ANY),
        scratch_shapes=[pltpu.VMEM((2, tme, D), F32), pltpu.SemaphoreType.DMA((2,)),
                        pltpu.VMEM((D, 2 * D_EXPERT), BF16), pltpu.VMEM((D_EXPERT, D), BF16),
                        pltpu.VMEM((W_SLOTS, W_CHUNK, 2 * D_EXPERT), F32), pltpu.VMEM((W_SLOTS, W_CHUNK, D), F32),
                        pltpu.SemaphoreType.DMA((2 * W_SLOTS,)),
                        pltpu.VMEM((2, tme, D // 2), U32), pltpu.SemaphoreType.DMA((2,))],
    )
    return pl.pallas_call(
        functools.partial(_moe_kernel, tme=tme, fc=256),
        grid_spec=grid_spec,
        out_shape=jax.ShapeDtypeStruct(((n_tiles + 2) * tme, D // 2), U32),
        compiler_params=_cparams(("arbitrary",)),
        name="moe",
    )(tile_expert, n_used, tok3, tok3, dst3, h2,
      w_gu, b_gu.reshape(N_EXPERTS, 1, 2 * D_EXPERT), w_d, b_d.reshape(N_EXPERTS, 1, D))


def _combine_kernel(y0_ref, y1_ref, y2_ref, y3_ref, gate_ref, x1_ref, mod_ref, o_ref):
    gates = gate_ref[...]
    n = o_ref.shape[1] // 2
    y_lo = y_hi = None
    for kk, y_ref in enumerate((y0_ref, y1_ref, y2_ref, y3_ref)):
        lo, hi = _unpack_halves(y_ref[...])
        g = gates[:, kk:kk + 1]
        y_lo = g * lo if y_lo is None else y_lo + g * lo
        y_hi = g * hi if y_hi is None else y_hi + g * hi
    g_f = mod_ref[0][5:6]
    o_ref[:, 0:n] = x1_ref[:, 0:n] + g_f[:, 0:n] * y_lo
    o_ref[:, n:] = x1_ref[:, n:] + g_f[:, n:] * y_hi


def _combine(y_slots, gates_t, x1, mod3, S, tm):
    T, D = x1.shape
    tps = S // tm
    n_tiles = T // tm
    slot_spec = lambda kk: pl.BlockSpec((tm, D // 2), lambda i: (kk * n_tiles + i, 0))
    return pl.pallas_call(
        _combine_kernel,
        grid=(n_tiles,),
        in_specs=[slot_spec(0), slot_spec(1), slot_spec(2), slot_spec(3),
                  pl.BlockSpec((tm, TOP_K), lambda i: (i, 0)),
                  pl.BlockSpec((tm, D), lambda i: (i, 0)),
                  pl.BlockSpec((1, 6, D), lambda i: (i // tps, 0, 0))],
        out_specs=pl.BlockSpec((tm, D), lambda i: (i, 0)),
        out_shape=jax.ShapeDtypeStruct((T, D), F32),
        compiler_params=_cparams(("arbitrary",)),
        name="combine",
    )(y_slots, y_slots, y_slots, y_slots, gates_t, x1, mod3)


def _route(top_i, tme):
    T = top_i.shape[1]
    TK = T * TOP_K
    assert TK % tme == 0
    flat_e = top_i.T.reshape(-1)
    counts = jnp.sum((flat_e[:, None] == jnp.arange(N_EXPERTS, dtype=jnp.int32)[None, :]).astype(jnp.int32), axis=0)
    padded = ((counts + tme - 1) // tme) * tme
    padded_end = jnp.cumsum(padded)
    padded_start = padded_end - padded
    group_end = jnp.cumsum(counts)
    group_start = group_end - counts

    n_tiles = TK // tme + N_EXPERTS
    tile_start = jnp.arange(n_tiles) * tme
    tile_expert = jnp.minimum(jnp.sum(tile_start[:, None] >= padded_end[None, :], axis=1),
                              N_EXPERTS - 1).astype(jnp.int32)
    n_used = (padded_end[-1] // tme).astype(jnp.int32).reshape(1)

    a_sorted = jnp.argsort(flat_e).astype(jnp.int32)
    p = tile_start[:, None] + jnp.arange(tme)[None, :]
    r = p - padded_start[tile_expert][:, None]
    valid = r < counts[tile_expert][:, None]
    src = jnp.clip(group_start[tile_expert][:, None] + r, 0, TK - 1)
    a = a_sorted[src.reshape(-1)].reshape(n_tiles, tme)
    buf_tok = jnp.where(valid, a // TOP_K, 0).astype(jnp.int32).reshape(-1)
    dst_rows = jnp.where(valid, (a % TOP_K) * T + a // TOP_K,
                         TK + p - group_end[tile_expert][:, None]).astype(jnp.int32).reshape(-1)
    return tile_expert, n_used, buf_tok, dst_rows


def _permute_weights(w_in, w_uq, w_ukv, q_head_g, k_head_g):
    half = QK_ROPE // 2
    pe0 = _C_PE
    pe_cols = jnp.concatenate([jnp.arange(pe0, pe0 + half), jnp.arange(pe0, pe0 + half),
                               jnp.arange(pe0 + half, pe0 + 2 * half), jnp.arange(pe0 + half, pe0 + 2 * half)])
    w_in_p = jnp.concatenate([w_in[:, :pe0], w_in[:, pe_cols]], axis=1).astype(BF16)

    hq = jnp.arange(MLA_HEADS)[:, None] * QK_HEAD
    nope_cols = (hq + jnp.arange(QK_NOPE)[None, :]).reshape(-1)
    rope_cols = []
    for j in range(MLA_HEADS // 2):
        for part in range(2):
            for h in (2 * j, 2 * j + 1):
                rope_cols.append(h * QK_HEAD + QK_NOPE + part * half + jnp.arange(half))
    wuq_p = w_uq[:, jnp.concatenate([nope_cols] + rope_cols)].astype(BF16)

    hk = jnp.arange(MLA_HEADS)[:, None] * (QK_NOPE + V_HEAD)
    kn_cols = (hk + jnp.arange(QK_NOPE)[None, :]).reshape(-1)
    v_cols = (hk + QK_NOPE + jnp.arange(V_HEAD)[None, :]).reshape(-1)
    wukv_p = w_ukv[:, jnp.concatenate([kn_cols, v_cols])].astype(BF16)

    def gain_p(g):
        x1, x2 = g[QK_NOPE:QK_NOPE + half], g[QK_NOPE + half:]
        return jnp.concatenate([g[:QK_NOPE], x1, x1, x2, x2]).reshape(1, 2 * LANES)

    return w_in_p, wuq_p, wukv_p, gain_p(q_head_g), gain_p(k_head_g)


def _tile(n, pref):
    t = pref
    while n % t:
        t //= 2
    return t


def kernel(x, c, positions, w_ada, b_ada, norm_mix_g, w_in, conv_w, q_lat_norm_g, w_uq, kv_lat_norm_g, w_ukv, q_head_norm_g, k_head_norm_g, conv_out_norm_g, mla_out_norm_g, w_out, norm_ffn_g, router_w, router_b, w_gate_up, b_gate_up, w_down, b_down):
    B, S, D = x.shape
    T = B * S
    assert D == D_MODEL and w_ada.shape[0] == 1 and S % 128 == 0
    l = 0
    tm = _tile(S, 512)
    ts = 512 if S >= 8192 else S // 8
    tme = 512 if T * TOP_K >= 32768 else 256
    tmc = _tile(S, 256)

    mod3 = _ada(c, w_ada[l], b_ada[l]).reshape(B, 6, D)
    x2 = x.reshape(T, D)

    w_in_p, wuq_p, wukv_p, qg_p, kg_p = _permute_weights(
        w_in[l], w_uq[l], w_ukv[l], q_head_norm_g[l], k_head_norm_g[l])
    half = QK_ROPE // 2
    inv_freq = ROPE_THETA ** (-jnp.arange(half, dtype=F32) / half)
    invf = jnp.tile(inv_freq, 4).reshape(1, LANES)

    mc, qln, kvln, kpe = _inproj(
        x2, mod3, norm_mix_g[l].reshape(1, D), w_in_p, conv_w[l],
        conv_out_norm_g[l].reshape(1, CONV_DIM), q_lat_norm_g[l].reshape(1, Q_LORA),
        kv_lat_norm_g[l].reshape(1, KV_LORA), S, tm)
    q, k, v = _upproj(qln, kvln, kpe, positions.reshape(T, 1), invf, wuq_p, wukv_p, qg_p, kg_p, B, S, tm)
    mm = _attention(q, k, v, mla_out_norm_g[l], B, S, ts, 4, 4 * ts).reshape(T, MLA_DIM)
    x1, h2, top_i, gates = _outproj(
        mc, mm, x2, mod3, w_out[l].astype(BF16), norm_ffn_g[l].reshape(1, D),
        router_w[l].T, router_b[l].reshape(N_EXPERTS, 1), S, tm)

    tile_expert, n_used, buf_tok, dst_rows = _route(top_i, tme)
    y_slots = _moe(tile_expert, n_used, buf_tok, dst_rows, h2, w_gate_up[l], b_gate_up[l], w_down[l], b_down[l], tme)
    out = _combine(y_slots, gates.T, x1, mod3, S, tmc)
    return out.reshape(B, S, D)
```
